```python
import math
import jax, jax.numpy as jnp
from jax import lax
import numpy as np

D_MODEL = 2048
BATCH = 4
SEQ = 8192
DEPTH = 4
DEC_BATCH = 32
DEC_SEQ = 16
PAST_LEN = 2048

CHUNK = 64
N_MIXERS = 2
N_SSD_LAYERS = (DEPTH + 1) // 2
N_MLSTM_LAYERS = DEPTH // 2
CONV_W = 4
SSD_INNER = 2 * D_MODEL
SSD_HEADDIM = 64
SSD_HEADS = SSD_INNER // SSD_HEADDIM
SSD_GROUPS = 8
SSD_HPG = SSD_HEADS // SSD_GROUPS
SSD_DSTATE = 128
SSD_CONV_DIM = SSD_INNER + 2 * SSD_GROUPS * SSD_DSTATE
SSD_PROJ = SSD_INNER + SSD_CONV_DIM + SSD_HEADS
ML_INNER = 2 * D_MODEL
ML_HEADS = 8
ML_HEADDIM = ML_INNER // ML_HEADS
ML_PROJ = 3 * ML_INNER + 2 * ML_HEADS
DEEPNORM_ALPHA = (2 * DEPTH) ** 0.25
DEEPNORM_BETA = (8 * DEPTH) ** -0.25
LN_EPS = 1e-5
RMS_EPS = 1e-6

kernel_name = "hybrid_ssd_mlstm_streaming_step"


def layer_norm(x, g, b):
    xf = x.astype(jnp.float32)
    mu = jnp.mean(xf, axis=-1, keepdims=True)
    var = jnp.mean(jnp.square(xf - mu), axis=-1, keepdims=True)
    return ((xf - mu) * lax.rsqrt(var + LN_EPS) * g + b).astype(x.dtype)


def causal_conv(u, conv_state, w, b):
    T = u.shape[1]
    up = jnp.concatenate([conv_state.astype(u.dtype), u], axis=1)
    y = b + up[:, 0:T] * w[0]
    for k in range(1, CONV_W):
        y = y + up[:, k:k + T] * w[k]
    return y, up[:, -(CONV_W - 1):]


def to_chunks(a, L):
    B, T = a.shape[0], a.shape[1]
    return jnp.moveaxis(a.reshape((B, T // L, L) + a.shape[2:]), 1, 0)


def from_chunks(a):
    a = jnp.moveaxis(a, 0, 1)
    return a.reshape((a.shape[0], a.shape[1] * a.shape[2]) + a.shape[3:])


def ssd_scan(x, dt, A, Bm, Cm, h0):
    T = x.shape[1]
    L = min(CHUNK, T)
    causal = jnp.tril(jnp.ones((L, L), dtype=bool))

    def step(h, inp):
        xc, dtc, Bc, Cc = inp
        cum = jnp.cumsum(dtc * A, axis=1)
        seg = cum[:, :, None] - cum[:, None, :]
        decay = jnp.exp(jnp.where(causal[None, :, :, None, None], seg, -jnp.inf))
        cb = jnp.einsum('btgn,bsgn->btsg', Cc, Bc)
        w = cb[..., None] * decay * dtc[:, None]
        y = jnp.einsum('btsgh,bsghp->btghp', w, xc)
        y = y + jnp.einsum('btgn,bghpn->btghp', Cc, h) * jnp.exp(cum)[..., None]
        tail = jnp.exp(cum[:, -1:] - cum) * dtc
        h = h * jnp.exp(cum[:, -1])[..., None, None] + jnp.einsum(
            'bsghp,bsgn->bghpn', xc * tail[..., None], Bc)
        return h, y

    hT, ys = lax.scan(step, h0, (to_chunks(x, L), to_chunks(dt, L), to_chunks(Bm, L), to_chunks(Cm, L)))
    return from_chunks(ys), hT


def mlstm_scan(q, k, v, li, lf, C0, n0, m0):
    T = q.shape[1]
    L = min(CHUNK, T)
    causal = jnp.tril(jnp.ones((L, L), dtype=bool))

    def step(carry, inp):
        C, n, m = carry
        qc, kc, vc, lic, lfc = inp
        b = jnp.cumsum(lfc, axis=1)
        dmat = b[:, :, None] - b[:, None, :] + lic[:, None]
        dmat = jnp.where(causal[None, :, :, None], dmat, -jnp.inf)
        inter = b + m[:, None]
        m_t = jnp.maximum(inter, jnp.max(dmat, axis=2))
        s = jnp.einsum('bthd,bshd->btsh', qc, kc) * jnp.exp(dmat - m_t[:, :, None])
        g = jnp.exp(inter - m_t)
        num = jnp.einsum('btsh,bshd->bthd', s, vc) + g[..., None] * jnp.einsum('bthk,bhkv->bthv', qc, C)
        den = jnp.sum(s, axis=2) + g * jnp.einsum('bthk,bhk->bth', qc, n)
        h = num / jnp.maximum(jnp.abs(den), jnp.exp(-m_t))[..., None]
        m_new = m_t[:, -1]
        tail = jnp.exp(b[:, -1:] - b + lic - m_new[:, None])
        dec = jnp.exp(b[:, -1] + m - m_new)
        kt = kc * tail[..., None]
        C = dec[..., None, None] * C + jnp.einsum('bshk,bshv->bhkv', kt, vc)
        n = dec[..., None] * n + jnp.sum(kt, axis=1)
        return (C, n, m_new), h

    (CT, nT, mT), hs = lax.scan(step, (C0, n0, m0),
                                (to_chunks(q, L), to_chunks(k, L), to_chunks(v, L),
                                 to_chunks(li, L), to_chunks(lf, L)))
    return from_chunks(hs), CT, nT, mT


def ssd_mixer(x, conv_state, h0, w_in, conv_w, conv_b, dt_bias, A_log, D, norm_w, w_out):
    Bsz, T = x.shape[0], x.shape[1]
    proj = x @ w_in
    z = proj[..., :SSD_INNER]
    xBC = proj[..., SSD_INNER:SSD_INNER + SSD_CONV_DIM]
    dt_raw = proj[..., SSD_INNER + SSD_CONV_DIM:]
    xBC, new_conv = causal_conv(xBC, conv_state, conv_w, conv_b)
    xBC = jax.nn.silu(xBC).astype(jnp.float32)
    xs = xBC[..., :SSD_INNER].reshape(Bsz, T, SSD_GROUPS, SSD_HPG, SSD_HEADDIM)
    Bm = xBC[..., SSD_INNER:SSD_INNER + SSD_GROUPS * SSD_DSTATE].reshape(Bsz, T, SSD_GROUPS, SSD_DSTATE)
    Cm = xBC[..., SSD_INNER + SSD_GROUPS * SSD_DSTATE:].reshape(Bsz, T, SSD_GROUPS, SSD_DSTATE)
    dt = jax.nn.softplus(dt_raw.astype(jnp.float32) + dt_bias).reshape(Bsz, T, SSD_GROUPS, SSD_HPG)
    A = -jnp.exp(A_log.astype(jnp.float32)).reshape(SSD_GROUPS, SSD_HPG)
    h0 = h0.astype(jnp.float32).reshape(Bsz, SSD_GROUPS, SSD_HPG, SSD_HEADDIM, SSD_DSTATE)
    y, hT = ssd_scan(xs, dt, A, Bm, Cm, h0)
    y = y + D.astype(jnp.float32).reshape(SSD_GROUPS, SSD_HPG)[:, :, None] * xs
    y = y.reshape(Bsz, T, SSD_GROUPS, -1) * jax.nn.silu(z.astype(jnp.float32)).reshape(Bsz, T, SSD_GROUPS, -1)
    y = y * lax.rsqrt(jnp.mean(jnp.square(y), axis=-1, keepdims=True) + RMS_EPS)
    y = y.reshape(Bsz, T, SSD_INNER) * norm_w
    out = y.astype(x.dtype) @ w_out
    return out, new_conv, hT.reshape(Bsz, SSD_HEADS, SSD_HEADDIM, SSD_DSTATE)


def mlstm_mixer(x, conv_state, C0, n0, m0, w_in, conv_w, conv_b, w_q, w_k, w_v, b_i, b_f, norm_w, skip, w_out):
    Bsz, T = x.shape[0], x.shape[1]
    proj = x @ w_in
    xm = proj[..., :ML_INNER]
    z = proj[..., ML_INNER:2 * ML_INNER]
    o_pre = proj[..., 2 * ML_INNER:3 * ML_INNER]
    gi = proj[..., 3 * ML_INNER:3 * ML_INNER + ML_HEADS]
    gf = proj[..., 3 * ML_INNER + ML_HEADS:]
    xc, new_conv = causal_conv(xm, conv_state, conv_w, conv_b)
    xc = jax.nn.silu(xc).astype(jnp.float32).reshape(Bsz, T, ML_HEADS, ML_HEADDIM)
    xmh = xm.astype(jnp.float32).reshape(Bsz, T, ML_HEADS, ML_HEADDIM)
    q = jnp.einsum('bthd,hde->bthe', xc, w_q) * (ML_HEADDIM ** -0.5)
    k = jnp.einsum('bthd,hde->bthe', xc, w_k)
    v = jnp.einsum('bthd,hde->bthe', xmh, w_v)
    li = gi.astype(jnp.float32) + b_i
    lf = jax.nn.log_sigmoid(gf.astype(jnp.float32) + b_f)
    h, CT, nT, mT = mlstm_scan(q, k, v, li, lf, C0.astype(jnp.float32),
                               n0.astype(jnp.float32), m0.astype(jnp.float32))
    mu = jnp.mean(h, axis=-1, keepdims=True)
    var = jnp.mean(jnp.square(h - mu), axis=-1, keepdims=True)
    hn = (h - mu) * lax.rsqrt(var + LN_EPS) * norm_w
    o = jax.nn.sigmoid(o_pre.astype(jnp.float32)).reshape(Bsz, T, ML_HEADS, ML_HEADDIM)
    hcell = (o * hn).reshape(Bsz, T, ML_INNER) + skip * xc.reshape(Bsz, T, ML_INNER)
    y = hcell * jax.nn.silu(z.astype(jnp.float32))
    out = y.astype(x.dtype) @ w_out
    return out, new_conv, CT, nT, mT


def trunk(x, states, weights):
    ssd_conv, ssd_h, ml_conv, ml_C, ml_n, ml_m = states
    (ssd_w_in, ssd_conv_w, ssd_conv_b, ssd_dt_bias, ssd_A_log, ssd_D, ssd_norm_w, ssd_w_out,
     ml_w_in, ml_conv_w, ml_conv_b, ml_w_q, ml_w_k, ml_w_v, ml_b_i, ml_b_f, ml_norm_w, ml_skip, ml_w_out,
     ln_g, ln_b) = weights
    o_sc, o_sh, o_mc, o_mC, o_mn, o_mm = [], [], [], [], [], []
    for i in range(DEPTH):
        j = i // N_MIXERS
        if i % N_MIXERS == 0:
            y, c, h = ssd_mixer(x, ssd_conv[j], ssd_h[j], ssd_w_in[j], ssd_conv_w[j], ssd_conv_b[j],
                                ssd_dt_bias[j], ssd_A_log[j], ssd_D[j], ssd_norm_w[j], ssd_w_out[j])
            o_sc.append(c)
            o_sh.append(h)
        else:
            y, c, C, n, m = mlstm_mixer(x, ml_conv[j], ml_C[j], ml_n[j], ml_m[j], ml_w_in[j], ml_conv_w[j],
                                        ml_conv_b[j], ml_w_q[j], ml_w_k[j], ml_w_v[j], ml_b_i[j], ml_b_f[j],
                                        ml_norm_w[j], ml_skip[j], ml_w_out[j])
            o_mc.append(c)
            o_mC.append(C)
            o_mn.append(n)
            o_mm.append(m)
        x = layer_norm(DEEPNORM_ALPHA * x + y, ln_g[i], ln_b[i])
    return (x, jnp.stack(o_sc), jnp.stack(o_sh), jnp.stack(o_mc), jnp.stack(o_mC),
            jnp.stack(o_mn), jnp.stack(o_mm))


def setup_inputs(seed: int = 0) -> dict:
    key = jax.random.key(seed)
    ks = iter(jax.random.split(key, 48))
    f32 = jnp.float32

    def nrm(shape, s):
        return jax.random.normal(next(ks), shape, f32) * s

    NA, NB = N_SSD_LAYERS, N_MLSTM_LAYERS
    dt0 = jnp.exp(jax.random.uniform(next(ks), (NA, SSD_HEADS), f32, math.log(1e-3), math.log(1e-1)))
    b_f = jnp.broadcast_to(jnp.linspace(3.0, 6.0, ML_HEADS, dtype=f32), (NB, ML_HEADS)) + nrm((NB, ML_HEADS), 0.1)
    return {
        "x_prompt": nrm((BATCH, SEQ, D_MODEL), 1.0),
        "x_sample": nrm((DEC_BATCH, DEC_SEQ, D_MODEL), 1.0),
        "state_ssd_conv": nrm((NA, DEC_BATCH, CONV_W - 1, SSD_CONV_DIM), 1.0),
        "state_ssd_h": nrm((NA, DEC_BATCH, SSD_HEADS, SSD_HEADDIM, SSD_DSTATE), 0.1),
        "state_mlstm_conv": nrm((NB, DEC_BATCH, CONV_W - 1, ML_INNER), 1.0),
        "state_mlstm_C": nrm((NB, DEC_BATCH, ML_HEADS, ML_HEADDIM, ML_HEADDIM), 0.1),
        "state_mlstm_n": nrm((NB, DEC_BATCH, ML_HEADS, ML_HEADDIM), 0.1),
        "state_mlstm_m": nrm((NB, DEC_BATCH, ML_HEADS), 1.0),
        "ssd_w_in": nrm((NA, D_MODEL, SSD_PROJ), D_MODEL ** -0.5),
        "ssd_conv_w": nrm((NA, CONV_W, SSD_CONV_DIM), CONV_W ** -0.5),
        "ssd_conv_b": nrm((NA, SSD_CONV_DIM), 0.01),
        "ssd_dt_bias": dt0 + jnp.log(-jnp.expm1(-dt0)),
        "ssd_A_log": jnp.log(jax.random.uniform(next(ks), (NA, SSD_HEADS), f32, 1.0, 16.0)),
        "ssd_D": 1.0 + nrm((NA, SSD_HEADS), 0.1),
        "ssd_norm_w": 1.0 + nrm((NA, SSD_INNER), 0.02),
        "ssd_w_out": nrm((NA, SSD_INNER, D_MODEL), SSD_INNER ** -0.5 * DEEPNORM_BETA),
        "ml_w_in": nrm((NB, D_MODEL, ML_PROJ), D_MODEL ** -0.5),
        "ml_conv_w": nrm((NB, CONV_W, ML_INNER), CONV_W ** -0.5),
        "ml_conv_b": nrm((NB, ML_INNER), 0.01),
        "ml_w_q": nrm((NB, ML_HEADS, ML_HEADDIM, ML_HEADDIM), ML_HEADDIM ** -0.5),
        "ml_w_k": nrm((NB, ML_HEADS, ML_HEADDIM, ML_HEADDIM), ML_HEADDIM ** -0.5),
        "ml_w_v": nrm((NB, ML_HEADS, ML_HEADDIM, ML_HEADDIM), ML_HEADDIM ** -0.5),
        "ml_b_i": nrm((NB, ML_HEADS), 0.1),
        "ml_b_f": b_f,
        "ml_norm_w": 1.0 + nrm((NB, ML_HEADS, ML_HEADDIM), 0.02),
        "ml_skip": 1.0 + nrm((NB, ML_INNER), 0.1),
        "ml_w_out": nrm((NB, ML_INNER, D_MODEL), ML_INNER ** -0.5 * DEEPNORM_BETA),
        "ln_g": 1.0 + nrm((DEPTH, D_MODEL), 0.01),
        "ln_b": nrm((DEPTH, D_MODEL), 0.01),
    }


def reference(x_prompt, x_sample, state_ssd_conv, state_ssd_h, state_mlstm_conv, state_mlstm_C,
              state_mlstm_n, state_mlstm_m,
              ssd_w_in, ssd_conv_w, ssd_conv_b, ssd_dt_bias, ssd_A_log, ssd_D, ssd_norm_w, ssd_w_out,
              ml_w_in, ml_conv_w, ml_conv_b, ml_w_q, ml_w_k, ml_w_v, ml_b_i, ml_b_f, ml_norm_w, ml_skip,
              ml_w_out, ln_g, ln_b):
    weights = (ssd_w_in, ssd_conv_w, ssd_conv_b, ssd_dt_bias, ssd_A_log, ssd_D, ssd_norm_w, ssd_w_out,
               ml_w_in, ml_conv_w, ml_conv_b, ml_w_q, ml_w_k, ml_w_v, ml_b_i, ml_b_f, ml_norm_w, ml_skip,
               ml_w_out, ln_g, ln_b)
    f32 = jnp.float32
    NA, NB = N_SSD_LAYERS, N_MLSTM_LAYERS
    zero_states = (jnp.zeros((NA, BATCH, CONV_W - 1, SSD_CONV_DIM), x_prompt.dtype),
                   jnp.zeros((NA, BATCH, SSD_HEADS, SSD_HEADDIM, SSD_DSTATE), f32),
                   jnp.zeros((NB, BATCH, CONV_W - 1, ML_INNER), x_prompt.dtype),
                   jnp.zeros((NB, BATCH, ML_HEADS, ML_HEADDIM, ML_HEADDIM), f32),
                   jnp.zeros((NB, BATCH, ML_HEADS, ML_HEADDIM), f32),
                   jnp.zeros((NB, BATCH, ML_HEADS), f32))
    y_prompt, p_sc, p_sh, p_mc, p_mC, p_mn, p_mm = trunk(x_prompt, zero_states, weights)
    cache_states = (state_ssd_conv, state_ssd_h, state_mlstm_conv, state_mlstm_C, state_mlstm_n, state_mlstm_m)
    y_sample, s_sc, s_sh, s_mc, s_mC, s_mn, s_mm = trunk(x_sample, cache_states, weights)
    return (y_prompt, y_sample, p_sc, p_sh, p_mc, p_mC, p_mn, p_mm, s_sc, s_sh, s_mc, s_mC, s_mn, s_mm)
```

```python
import functools

import jax
import jax.numpy as jnp
from jax import lax
from jax.experimental import pallas as pl
from jax.experimental.pallas import tpu as pltpu

F32 = jnp.float32
BF16 = jnp.bfloat16

D_MODEL = 2048
DEPTH = 4
CHUNK = 64
CONV_W = 4
SSD_INNER = 4096
SSD_P = 64
SSD_H = 64
SSD_G = 8
SSD_N = 128
SSD_CONV = SSD_INNER + 2 * SSD_G * SSD_N
ML_INNER = 4096
ML_H = 8
ML_D = 512
ALPHA = (2 * DEPTH) ** 0.25
LN_EPS = 1e-5
RMS_EPS = 1e-6
NEG = -1e30

LANE = 128
VMEM_LIMIT = 56 * 1024 * 1024
CARRY0 = 8 - (CONV_W - 1)


def _params(*sem):
    return pltpu.CompilerParams(dimension_semantics=sem, vmem_limit_bytes=VMEM_LIMIT)


def _dot(a, b):
    return jnp.dot(a, b, preferred_element_type=F32)


def _dot_nt(a, b):
    return lax.dot_general(a, b, (((1,), (1,)), ((), ())), preferred_element_type=F32)


def _dot_tn(a, b):
    return lax.dot_general(a, b, (((0,), (0,)), ((), ())), preferred_element_type=F32)


def _split(x, n):
    out = []
    r = x
    for i in range(n):
        p = r.astype(BF16)
        out.append(p)
        if i + 1 < n:
            r = r - p.astype(F32)
    return out


def _silu(x):
    return x * jax.nn.sigmoid(x)


def _softplus(x):
    return jnp.maximum(x, 0.0) + jnp.log1p(jnp.exp(-jnp.abs(x)))


def _mm_body(a_ref, w_ref, o_ref):
    o_ref[...] = _dot(a_ref[...], w_ref[...]).astype(o_ref.dtype)


def _matmul(a, w, out_dtype=F32):
    M, K = a.shape
    N = w.shape[1]
    tm = min(M, 1024)
    tn = min(N, 512)
    assert M % tm == 0 and N % tn == 0
    return pl.pallas_call(
        _mm_body,
        grid=(M // tm, N // tn),
        in_specs=[pl.BlockSpec((tm, K), lambda i, j: (i, 0)),
                  pl.BlockSpec((K, tn), lambda i, j: (0, j))],
        out_specs=pl.BlockSpec((tm, tn), lambda i, j: (i, j)),
        out_shape=jax.ShapeDtypeStruct((M, N), out_dtype),
        compiler_params=_params("parallel", "arbitrary"),
    )(a, w)


def _out_body(y_ref, w_ref, x_ref, g_ref, b_ref, of_ref, ob_ref, acc_ref, *, nk):
    k = pl.program_id(1)

    @pl.when(k == 0)
    def _():
        acc_ref[...] = jnp.zeros_like(acc_ref)

    acc_ref[...] += _dot(y_ref[...], w_ref[...])

    @pl.when(k == nk - 1)
    def _():
        r = ALPHA * x_ref[...] + acc_ref[...]
        mu = jnp.mean(r, axis=-1, keepdims=True)
        d = r - mu
        var = jnp.mean(d * d, axis=-1, keepdims=True)
        o = d * lax.rsqrt(var + LN_EPS) * g_ref[...] + b_ref[...]
        of_ref[...] = o
        ob_ref[...] = o.astype(BF16)


def _out_ln(y, w, x, g, b):
    M, K = y.shape
    N = w.shape[1]
    tm = min(M, 512)
    tk = min(K, 1024)
    nk = K // tk
    return pl.pallas_call(
        functools.partial(_out_body, nk=nk),
        grid=(M // tm, nk),
        in_specs=[pl.BlockSpec((tm, tk), lambda i, k: (i, k)),
                  pl.BlockSpec((tk, N), lambda i, k: (k, 0)),
                  pl.BlockSpec((tm, N), lambda i, k: (i, 0)),
                  pl.BlockSpec((1, N), lambda i, k: (0, 0)),
                  pl.BlockSpec((1, N), lambda i, k: (0, 0))],
        out_specs=[pl.BlockSpec((tm, N), lambda i, k: (i, 0)),
                   pl.BlockSpec((tm, N), lambda i, k: (i, 0))],
        out_shape=[jax.ShapeDtypeStruct((M, N), F32), jax.ShapeDtypeStruct((M, N), BF16)],
        scratch_shapes=[pltpu.VMEM((tm, N), F32)],
        compiler_params=_params("parallel", "arbitrary"),
    )(y, w, x, g.reshape(1, N), b.reshape(1, N))


def _ssd_body(xbc_ref, z_ref, dt_ref, conv0_ref, h0_ref, cw_ref, cbias_ref, dtb_ref, alog_ref,
              de_ref, nw_ref, e3_ref, tril_ref,
              y_ref, ht_ref, convt_ref,
              ext_ref, st_ref, ybuf_ref, *, nsteps, nchunks, valid):
    L = CHUNK
    W = SSD_INNER
    step = pl.program_id(1)
    total = nsteps * nchunks * L

    @pl.when(step == 0)
    def _():
        st_ref[...] = h0_ref[0].T
        ext_ref[CARRY0:8, :] = conv0_ref[0]

    a_neg = -jnp.exp(alog_ref[...])

    def chunk(c, carry):
        r0 = pl.multiple_of(c * L, L)
        ext_ref[8:8 + L, :] = xbc_ref[0, pl.ds(r0, L), :]
        acc = cbias_ref[...] + ext_ref[CARRY0:CARRY0 + L, :] * cw_ref[0:1, :]
        for k in range(1, CONV_W):
            acc = acc + ext_ref[CARRY0 + k:CARRY0 + k + L, :] * cw_ref[k:k + 1, :]
        ext_ref[CARRY0:8, :] = ext_ref[CARRY0 + L:8 + L, :]
        xbc = _silu(acc)
        xs = xbc[:, :W]
        bm = xbc[:, W:W + SSD_G * SSD_N].astype(BF16)
        cm = xbc[:, W + SSD_G * SSD_N:].astype(BF16)

        dt = _softplus(dt_ref[0, pl.ds(r0, L), :] + dtb_ref[...])
        if valid < total:
            rows = lax.broadcasted_iota(jnp.int32, (L, SSD_H), 0) + (step * nchunks + c) * L
            dt = jnp.where(rows < valid, dt, 0.0)
        a = dt * a_neg
        tril = tril_ref[...]
        cum = sum(_dot(tril, p) for p in _split(a, 3))

        cum_e = _dot(jnp.concatenate(_split(cum, 3), axis=1), e3_ref[...])
        dt_e = _dot(jnp.concatenate(_split(dt, 2), axis=1), e3_ref[0:2 * SSD_H, :])

        row = lax.broadcasted_iota(jnp.int32, (L, W), 0)
        lane = lax.broadcasted_iota(jnp.int32, (L, W), 1)
        s_idx = lane & (L - 1)
        cum_flat = jnp.sum(jnp.where(s_idx == row, cum_e, 0.0), axis=0, keepdims=True)
        decay = jnp.exp(jnp.where(s_idx <= row, cum_e - cum_flat, -jnp.inf))

        xdt = xs * dt_e
        even = (lane & L) == 0
        xdt_top = jnp.where(even, xdt, 0.0).astype(BF16)
        xdt_bot = jnp.where(even, 0.0, xdt).astype(BF16)
        cum_last = cum_e[L - 1:L, :]
        exp_cum = jnp.exp(cum_e)
        xt = (xs * (jnp.exp(cum_last - cum_e) * dt_e)).astype(BF16)
        d_last = jnp.exp(cum_last)

        for g in range(SSD_G):
            gs = slice(g * SSD_N, (g + 1) * SSD_N)
            hs = slice(g * 512, (g + 1) * 512)
            cg = cm[:, gs]
            bg = bm[:, gs]
            cb2 = _dot_nt(cg, jnp.concatenate([bg, bg], axis=0))
            hg = st_ref[:, hs]
            ys = _dot(cg, hg.astype(BF16))
            for pr in range(4):
                lo = g * 512 + pr * LANE
                ps = slice(lo, lo + LANE)
                w = (cb2 * decay[:, ps]).astype(BF16)
                xblk = jnp.concatenate([xdt_top[:, ps], xdt_bot[:, ps]], axis=0)
                ybuf_ref[:, ps] = _dot(w, xblk) + ys[:, pr * LANE:(pr + 1) * LANE] * exp_cum[:, ps]
            st_ref[:, hs] = hg * d_last[:, hs] + _dot_tn(bg, xt[:, hs])

        y = ybuf_ref[...] + de_ref[...] * xs
        yg = y * _silu(z_ref[0, pl.ds(r0, L), :])
        for g in range(SSD_G):
            hs = slice(g * 512, (g + 1) * 512)
            blk = yg[:, hs]
            ms = jnp.mean(blk * blk, axis=-1, keepdims=True)
            y_ref[0, pl.ds(r0, L), hs] = (blk * lax.rsqrt(ms + RMS_EPS) * nw_ref[:, hs]).astype(BF16)
        return carry

    lax.fori_loop(0, nchunks, chunk, 0)

    @pl.when(step == nsteps - 1)
    def _():
        ht_ref[0] = st_ref[...].T
        vl = valid - (total - L)
        convt_ref[0] = ext_ref[8 + vl - (CONV_W - 1):8 + vl, :]


def _ssd_scan(xbc, z, dt, conv0, h0, p, valid):
    B, T, _ = xbc.shape
    L = CHUNK
    assert SSD_P == L and T % L == 0 and valid > T - L and valid >= CONV_W - 1
    tb = min(T, 2 * L)
    nsteps = T // tb
    nchunks = tb // L
    W = SSD_INNER
    eye = jnp.repeat(jnp.eye(SSD_H, dtype=BF16), SSD_P, axis=1)
    e3 = jnp.concatenate([eye, eye, eye], axis=0)
    tril = jnp.tril(jnp.ones((L, L), BF16))
    full = lambda shape: pl.BlockSpec(shape, lambda b, s: (0,) * len(shape))
    y, ht, convt = pl.pallas_call(
        functools.partial(_ssd_body, nsteps=nsteps, nchunks=nchunks, valid=valid),
        grid=(B, nsteps),
        in_specs=[pl.BlockSpec((1, tb, SSD_CONV), lambda b, s: (b, s, 0)),
                  pl.BlockSpec((1, tb, W), lambda b, s: (b, s, 0)),
                  pl.BlockSpec((1, tb, SSD_H), lambda b, s: (b, s, 0)),
                  pl.BlockSpec((1, CONV_W - 1, SSD_CONV), lambda b, s: (b, 0, 0)),
                  pl.BlockSpec((1, W, SSD_N), lambda b, s: (b, 0, 0)),
                  full((CONV_W, SSD_CONV)), full((1, SSD_CONV)), full((1, SSD_H)), full((1, SSD_H)),
                  full((1, W)), full((1, W)), full((3 * SSD_H, W)), full((L, L))],
        out_specs=[pl.BlockSpec((1, tb, W), lambda b, s: (b, s, 0)),
                   pl.BlockSpec((1, W, SSD_N), lambda b, s: (b, 0, 0)),
                   pl.BlockSpec((1, CONV_W - 1, SSD_CONV), lambda b, s: (b, 0, 0))],
        out_shape=[jax.ShapeDtypeStruct((B, T, W), BF16),
                   jax.ShapeDtypeStruct((B, W, SSD_N), F32),
                   jax.ShapeDtypeStruct((B, CONV_W - 1, SSD_CONV), F32)],
        scratch_shapes=[pltpu.VMEM((8 + L, SSD_CONV), F32),
                        pltpu.VMEM((SSD_N, W), F32),
                        pltpu.VMEM((L, W), F32)],
        compiler_params=_params("parallel", "arbitrary"),
    )(xbc, z, dt, conv0, h0.reshape(B, W, SSD_N),
      p["conv_w"], p["conv_b"].reshape(1, -1), p["dt_bias"].reshape(1, -1), p["A_log"].reshape(1, -1),
      jnp.repeat(p["D"], SSD_P).reshape(1, W), p["norm_w"].reshape(1, W), e3, tril)
    return y, ht.reshape(B, SSD_H, SSD_P, SSD_N), convt


def _ml_body(xm_ref, z_ref, o_ref, g_ref, gt_ref, conv0_ref, c0_ref, n0_ref, m0_ref, cw_ref, cbias_ref,
             wq_ref, wk_ref, wv_ref, gb_ref, gbt_ref, nw_ref, skip_ref, tril_ref, triu_ref,
             y_ref, ct_ref, nt_ref, mt_ref, convt_ref,
             ext_ref, q_s, k_s, v_s, xc_s, *, nsteps, nchunks, valid):
    L = CHUNK
    tb = nchunks * L
    total = nsteps * tb
    head = pl.program_id(0)
    step = pl.program_id(2)

    @pl.when(step == 0)
    def _():
        ct_ref[...] = c0_ref[...]
        nt_ref[...] = n0_ref[...]
        mt_ref[...] = m0_ref[...]
        ext_ref[CARRY0:8, :] = conv0_ref[0]

    xm = xm_ref[0]
    ext_ref[8:8 + tb, :] = xm
    acc = cbias_ref[...] + ext_ref[CARRY0:CARRY0 + tb, :] * cw_ref[0:1, :]
    for k in range(1, CONV_W):
        acc = acc + ext_ref[CARRY0 + k:CARRY0 + k + tb, :] * cw_ref[k:k + 1, :]
    ext_ref[CARRY0:8, :] = ext_ref[CARRY0 + tb:8 + tb, :]
    xc = _silu(acc)
    xc_s[...] = xc
    xcb = xc.astype(BF16)
    q_s[...] = _dot(xcb, wq_ref[0]) * (ML_D ** -0.5)
    k_s[...] = _dot(xcb, wk_ref[0])
    v_s[...] = _dot(xm.astype(BF16), wv_ref[0]).astype(BF16)

    @pl.when(step == nsteps - 1)
    def _():
        vl = valid - (total - tb)
        convt_ref[0] = ext_ref[8 + vl - (CONV_W - 1):8 + vl, :]

    def chunk(c, carry):
        r0 = pl.multiple_of(c * L, L)
        q = q_s[pl.ds(r0, L), :]
        k = k_s[pl.ds(r0, L), :]
        v = v_s[pl.ds(r0, L), :]
        qb = q.astype(BF16)

        gc = g_ref[0, pl.ds(r0, L), :] + gb_ref[...]
        li_c = gc
        lf_c = -_softplus(-gc)
        lane16 = lax.broadcasted_iota(jnp.int32, (L, 2 * ML_H), 1)
        if valid < total:
            rows = lax.broadcasted_iota(jnp.int32, (L, 2 * ML_H), 0) + (step * nchunks + c) * L
            li_c = jnp.where(rows < valid, li_c, NEG)
            lf_c = jnp.where(rows < valid, lf_c, 0.0)
        tril = tril_ref[...]
        b_c = sum(_dot(tril, p) for p in _split(lf_c, 3))
        li_col = jnp.sum(jnp.where(lane16 == head, li_c, 0.0), axis=1, keepdims=True)
        b_col = jnp.sum(jnp.where(lane16 == head + ML_H, b_c, 0.0), axis=1, keepdims=True)

        gr = gt_ref[0, c] + gbt_ref[...]
        li_r = gr
        lf_r = -_softplus(-gr)
        sub16 = lax.broadcasted_iota(jnp.int32, (2 * ML_H, L), 0)
        if valid < total:
            cols = lax.broadcasted_iota(jnp.int32, (2 * ML_H, L), 1) + (step * nchunks + c) * L
            li_r = jnp.where(cols < valid, li_r, NEG)
            lf_r = jnp.where(cols < valid, lf_r, 0.0)
        triu = triu_ref[...]
        b_r = sum(_dot(p, triu) for p in _split(lf_r, 3))
        li_row = jnp.sum(jnp.where(sub16 == head, li_r, 0.0), axis=0, keepdims=True)
        b_row = jnp.sum(jnp.where(sub16 == head + ML_H, b_r, 0.0), axis=0, keepdims=True)

        m_prev = mt_ref[0, 0]
        t_i = lax.broadcasted_iota(jnp.int32, (L, L), 0)
        s_i = lax.broadcasted_iota(jnp.int32, (L, L), 1)
        dm = jnp.where(s_i <= t_i, b_col - b_row + li_row, -jnp.inf)
        inter = b_col + m_prev
        m_t = jnp.maximum(inter, jnp.max(dm, axis=1, keepdims=True))
        s = _dot_nt(qb, k.astype(BF16)) * jnp.exp(dm - m_t)
        gg = jnp.exp(inter - m_t)
        c_prev = ct_ref[0, 0]
        n_prev = nt_ref[0, 0]
        num = _dot(s.astype(BF16), v) + gg * _dot(qb, c_prev.astype(BF16))
        den = jnp.sum(s, axis=1, keepdims=True) + gg * jnp.sum(q * n_prev, axis=1, keepdims=True)
        hh = num / jnp.maximum(jnp.abs(den), jnp.exp(-m_t))
        m_new = m_t[L - 1:L, :]
        b_last = b_col[L - 1:L, :]
        tail = jnp.exp(b_last - b_col + li_col - m_new)
        dec = jnp.exp(b_last + m_prev - m_new)
        kt = k * tail
        ct_ref[0, 0] = dec * c_prev + _dot_tn(kt.astype(BF16), v)
        nt_ref[0, 0] = dec * n_prev + jnp.sum(kt, axis=0, keepdims=True)
        mt_ref[0, 0] = m_new

        mu = jnp.mean(hh, axis=-1, keepdims=True)
        d = hh - mu
        var = jnp.mean(d * d, axis=-1, keepdims=True)
        hn = d * lax.rsqrt(var + LN_EPS) * nw_ref[0]
        og = jax.nn.sigmoid(o_ref[0, pl.ds(r0, L), :])
        hc = og * hn + skip_ref[...] * xc_s[pl.ds(r0, L), :]
        y_ref[0, pl.ds(r0, L), :] = (hc * _silu(z_ref[0, pl.ds(r0, L), :])).astype(BF16)
        return carry

    lax.fori_loop(0, nchunks, chunk, 0)


def _ml_scan(proj, gates, conv0, c0, n0, m0, p, valid):
    B, T, _ = proj.shape
    L = CHUNK
    D = ML_D
    H = ML_H
    tb = min(T, 8 * L)
    assert T % tb == 0 and valid > T - L and valid >= CONV_W - 1
    nsteps = T // tb
    nchunks = tb // L
    gates_t = gates.reshape(B, T // L, L, 2 * H).transpose(0, 1, 3, 2)
    gb = jnp.concatenate([p["b_i"], p["b_f"]]).reshape(1, 2 * H)
    tril = jnp.tril(jnp.ones((L, L), BF16))
    const = lambda shape: pl.BlockSpec(shape, lambda h, b, s: (0,) * len(shape))
    y, ct, nt, mt, convt = pl.pallas_call(
        functools.partial(_ml_body, nsteps=nsteps, nchunks=nchunks, valid=valid),
        grid=(H, B, nsteps),
        in_specs=[pl.BlockSpec((1, tb, D), lambda h, b, s: (b, s, h)),
                  pl.BlockSpec((1, tb, D), lambda h, b, s: (b, s, H + h)),
                  pl.BlockSpec((1, tb, D), lambda h, b, s: (b, s, 2 * H + h)),
                  pl.BlockSpec((1, tb, 2 * H), lambda h, b, s: (b, s, 0)),
                  pl.BlockSpec((1, nchunks, 2 * H, L), lambda h, b, s: (b, s, 0, 0)),
                  pl.BlockSpec((1, CONV_W - 1, D), lambda h, b, s: (b, 0, h)),
                  pl.BlockSpec((1, 1, D, D), lambda h, b, s: (b, h, 0, 0)),
                  pl.BlockSpec((1, 1, 1, D), lambda h, b, s: (b, h, 0, 0)),
                  pl.BlockSpec((1, 1, 1, 1), lambda h, b, s: (b, h, 0, 0)),
                  pl.BlockSpec((CONV_W, D), lambda h, b, s: (0, h)),
                  pl.BlockSpec((1, D), lambda h, b, s: (0, h)),
                  pl.BlockSpec((1, D, D), lambda h, b, s: (h, 0, 0)),
                  pl.BlockSpec((1, D, D), lambda h, b, s: (h, 0, 0)),
                  pl.BlockSpec((1, D, D), lambda h, b, s: (h, 0, 0)),
                  const((1, 2 * H)), const((2 * H, 1)),
                  pl.BlockSpec((1, 1, D), lambda h, b, s: (h, 0, 0)),
                  pl.BlockSpec((1, D), lambda h, b, s: (0, h)),
                  const((L, L)), const((L, L))],
        out_specs=[pl.BlockSpec((1, tb, D), lambda h, b, s: (b, s, h)),
                   pl.BlockSpec((1, 1, D, D), lambda h, b, s: (b, h, 0, 0)),
                   pl.BlockSpec((1, 1, 1, D), lambda h, b, s: (b, h, 0, 0)),
                   pl.BlockSpec((1, 1, 1, 1), lambda h, b, s: (b, h, 0, 0)),
                   pl.BlockSpec((1, CONV_W - 1, D), lambda h, b, s: (b, 0, h))],
        out_shape=[jax.ShapeDtypeStruct((B, T, ML_INNER), BF16),
                   jax.ShapeDtypeStruct((B, H, D, D), F32),
                   jax.ShapeDtypeStruct((B, H, 1, D), F32),
                   jax.ShapeDtypeStruct((B, H, 1, 1), F32),
                   jax.ShapeDtypeStruct((B, CONV_W - 1, ML_INNER), F32)],
        scratch_shapes=[pltpu.VMEM((8 + tb, D), F32),
                        pltpu.VMEM((tb, D), F32),
                        pltpu.VMEM((tb, D), F32),
                        pltpu.VMEM((tb, D), BF16),
                        pltpu.VMEM((tb, D), F32)],
        compiler_params=_params("parallel", "parallel", "arbitrary"),
    )(proj, proj, proj, gates, gates_t, conv0, c0, n0.reshape(B, H, 1, D), m0.reshape(B, H, 1, 1),
      p["conv_w"], p["conv_b"].reshape(1, -1), p["w_q"], p["w_k"], p["w_v"],
      gb, gb.reshape(2 * H, 1), p["norm_w"].reshape(H, 1, D), p["skip"].reshape(1, -1), tril, tril.T)
    return y, ct, nt.reshape(B, H, D), mt.reshape(B, H), convt


def _pad_frames(a, tp):
    t = a.shape[1]
    return a if t == tp else jnp.pad(a, ((0, 0), (0, tp - t), (0, 0)))


def _ssd_layer(xf, xb, conv0, h0, p, ln_g, ln_b):
    B, T, D = xf.shape
    M = B * T
    tp = -(-T // CHUNK) * CHUNK
    xb2 = xb.reshape(M, D)
    z = _pad_frames(_matmul(xb2, p["w_z"]).reshape(B, T, -1), tp)
    xbc = _pad_frames(_matmul(xb2, p["w_xbc"]).reshape(B, T, -1), tp)
    dt = _pad_frames(_matmul(xb2, p["w_dt"]).reshape(B, T, -1), tp)
    y, ht, convt = _ssd_scan(xbc, z, dt, conv0, h0, p, T)
    of, ob = _out_ln(y[:, :T].reshape(M, -1), p["w_out"], xf.reshape(M, D), ln_g, ln_b)
    return of.reshape(B, T, D), ob.reshape(B, T, D), convt, ht


def _ml_layer(xf, xb, conv0, c0, n0, m0, p, ln_g, ln_b):
    B, T, D = xf.shape
    M = B * T
    tp = -(-T // CHUNK) * CHUNK
    xb2 = xb.reshape(M, D)
    proj = _pad_frames(_matmul(xb2, p["w_main"]).reshape(B, T, -1), tp)
    gates = _pad_frames(_matmul(xb2, p["w_gate"]).reshape(B, T, -1), tp)
    y, ct, nt, mt, convt = _ml_scan(proj, gates, conv0, c0, n0, m0, p, T)
    of, ob = _out_ln(y[:, :T].reshape(M, -1), p["w_out"], xf.reshape(M, D), ln_g, ln_b)
    return of.reshape(B, T, D), ob.reshape(B, T, D), convt, ct, nt, mt


def _trunk(x, states, ssd_p, ml_p, ln_g, ln_b):
    ssd_conv, ssd_h, ml_conv, ml_c, ml_n, ml_m = states
    xf, xb = x, x.astype(BF16)
    o_sc, o_sh, o_mc, o_mC, o_mn, o_mm = [], [], [], [], [], []
    for i in range(DEPTH):
        j = i // 2
        if i % 2 == 0:
            xf, xb, c, h = _ssd_layer(xf, xb, ssd_conv[j], ssd_h[j], ssd_p[j], ln_g[i], ln_b[i])
            o_sc.append(c)
            o_sh.append(h)
        else:
            xf, xb, c, C, n, m = _ml_layer(xf, xb, ml_conv[j], ml_c[j], ml_n[j], ml_m[j], ml_p[j],
                                           ln_g[i], ln_b[i])
            o_mc.append(c)
            o_mC.append(C)
            o_mn.append(n)
            o_mm.append(m)
    return (xf, jnp.stack(o_sc), jnp.stack(o_sh), jnp.stack(o_mc), jnp.stack(o_mC),
            jnp.stack(o_mn), jnp.stack(o_mm))


def kernel(x_prompt, x_sample, state_ssd_conv, state_ssd_h, state_mlstm_conv, state_mlstm_C, state_mlstm_n, state_mlstm_m, ssd_w_in, ssd_conv_w, ssd_conv_b, ssd_dt_bias, ssd_A_log, ssd_D, ssd_norm_w, ssd_w_out, ml_w_in, ml_conv_w, ml_conv_b, ml_w_q, ml_w_k, ml_w_v, ml_b_i, ml_b_f, ml_norm_w, ml_skip, ml_w_out, ln_g, ln_b):
    na, nb = ssd_w_in.shape[0], ml_w_in.shape[0]
    ssd_p = []
    for j in range(na):
        w = ssd_w_in[j].astype(BF16)
        ssd_p.append(dict(
            w_z=w[:, :SSD_INNER], w_xbc=w[:, SSD_INNER:SSD_INNER + SSD_CONV], w_dt=w[:, SSD_INNER + SSD_CONV:],
            conv_w=ssd_conv_w[j], conv_b=ssd_conv_b[j], dt_bias=ssd_dt_bias[j], A_log=ssd_A_log[j],
            D=ssd_D[j], norm_w=ssd_norm_w[j], w_out=ssd_w_out[j].astype(BF16)))
    ml_p = []
    for j in range(nb):
        w = ml_w_in[j].astype(BF16)
        ml_p.append(dict(
            w_main=w[:, :3 * ML_INNER], w_gate=w[:, 3 * ML_INNER:],
            conv_w=ml_conv_w[j], conv_b=ml_conv_b[j],
            w_q=ml_w_q[j].astype(BF16), w_k=ml_w_k[j].astype(BF16), w_v=ml_w_v[j].astype(BF16),
            b_i=ml_b_i[j], b_f=ml_b_f[j], norm_w=ml_norm_w[j], skip=ml_skip[j],
            w_out=ml_w_out[j].astype(BF16)))

    bp = x_prompt.shape[0]
    zero_states = (jnp.zeros((na, bp, CONV_W - 1, SSD_CONV), F32),
                   jnp.zeros((na, bp, SSD_H, SSD_P, SSD_N), F32),
                   jnp.zeros((nb, bp, CONV_W - 1, ML_INNER), F32),
                   jnp.zeros((nb, bp, ML_H, ML_D, ML_D), F32),
                   jnp.zeros((nb, bp, ML_H, ML_D), F32),
                   jnp.zeros((nb, bp, ML_H), F32))
    yp, p_sc, p_sh, p_mc, p_mC, p_mn, p_mm = _trunk(x_prompt, zero_states, ssd_p, ml_p, ln_g, ln_b)
    cache = (state_ssd_conv, state_ssd_h, state_mlstm_conv, state_mlstm_C, state_mlstm_n, state_mlstm_m)
    ys, s_sc, s_sh, s_mc, s_mC, s_mn, s_mm = _trunk(x_sample, cache, ssd_p, ml_p, ln_g, ln_b)
    return (yp, ys, p_sc, p_sh, p_mc, p_mC, p_mn, p_mm, s_sc, s_sh, s_mc, s_mC, s_mn, s_mm)
```

```python
import functools

import jax
import jax.numpy as jnp
from jax import lax
from jax.experimental import pallas as pl
from jax.experimental.pallas import tpu as pltpu

F32 = jnp.float32
BF16 = jnp.bfloat16

D_MODEL = 2048
DEPTH = 4
CHUNK = 64
ML_CHUNK = 256
ML_BLOCK = 512
CONV_W = 4
SSD_INNER = 4096
SSD_P = 64
SSD_H = 64
SSD_G = 8
SSD_N = 128
SSD_CONV = SSD_INNER + 2 * SSD_G * SSD_N
ML_INNER = 4096
ML_H = 8
ML_D = 512
ALPHA = (2 * DEPTH) ** 0.25
LN_EPS = 1e-5
RMS_EPS = 1e-6
NEG = -1e30

LANE = 128
VMEM_LIMIT = 56 * 1024 * 1024
CARRY0 = 8 - (CONV_W - 1)


def _params(*sem):
    return pltpu.CompilerParams(dimension_semantics=sem, vmem_limit_bytes=VMEM_LIMIT)


def _dot(a, b):
    return jnp.dot(a, b, preferred_element_type=F32)


def _dot_nt(a, b):
    return lax.dot_general(a, b, (((1,), (1,)), ((), ())), preferred_element_type=F32)


def _dot_tn(a, b):
    return lax.dot_general(a, b, (((0,), (0,)), ((), ())), preferred_element_type=F32)


def _split(x, n):
    out = []
    r = x
    for i in range(n):
        p = r.astype(BF16)
        out.append(p)
        if i + 1 < n:
            r = r - p.astype(F32)
    return out


def _silu(x):
    return x * jax.nn.sigmoid(x)


def _softplus(x):
    return jnp.maximum(x, 0.0) + jnp.log1p(jnp.exp(-jnp.abs(x)))


def _mm_body(a_ref, w_ref, o_ref):
    o_ref[...] = _dot(a_ref[...], w_ref[...]).astype(o_ref.dtype)


def _matmul(a, w, out_dtype=F32):
    M, K = a.shape
    N = w.shape[1]
    tm = min(M, 1024)
    tn = min(N, 512)
    assert M % tm == 0 and N % tn == 0
    return pl.pallas_call(
        _mm_body,
        grid=(M // tm, N // tn),
        in_specs=[pl.BlockSpec((tm, K), lambda i, j: (i, 0)),
                  pl.BlockSpec((K, tn), lambda i, j: (0, j))],
        out_specs=pl.BlockSpec((tm, tn), lambda i, j: (i, j)),
        out_shape=jax.ShapeDtypeStruct((M, N), out_dtype),
        name="in_proj",
        compiler_params=_params("parallel", "arbitrary"),
    )(a, w)


def _out_body(y_ref, w_ref, x_ref, g_ref, b_ref, of_ref, ob_ref, acc_ref, *, nk):
    k = pl.program_id(1)

    @pl.when(k == 0)
    def _():
        acc_ref[...] = jnp.zeros_like(acc_ref)

    acc_ref[...] += _dot(y_ref[...], w_ref[...])

    @pl.when(k == nk - 1)
    def _():
        r = ALPHA * x_ref[...] + acc_ref[...]
        mu = jnp.mean(r, axis=-1, keepdims=True)
        d = r - mu
        var = jnp.mean(d * d, axis=-1, keepdims=True)
        o = d * lax.rsqrt(var + LN_EPS) * g_ref[...] + b_ref[...]
        of_ref[...] = o
        ob_ref[...] = o.astype(BF16)


def _out_ln(y, w, x, g, b):
    M, K = y.shape
    N = w.shape[1]
    tm = min(M, 512)
    tk = min(K, 1024)
    nk = K // tk
    return pl.pallas_call(
        functools.partial(_out_body, nk=nk),
        grid=(M // tm, nk),
        in_specs=[pl.BlockSpec((tm, tk), lambda i, k: (i, k)),
                  pl.BlockSpec((tk, N), lambda i, k: (k, 0)),
                  pl.BlockSpec((tm, N), lambda i, k: (i, 0)),
                  pl.BlockSpec((1, N), lambda i, k: (0, 0)),
                  pl.BlockSpec((1, N), lambda i, k: (0, 0))],
        out_specs=[pl.BlockSpec((tm, N), lambda i, k: (i, 0)),
                   pl.BlockSpec((tm, N), lambda i, k: (i, 0))],
        out_shape=[jax.ShapeDtypeStruct((M, N), F32), jax.ShapeDtypeStruct((M, N), BF16)],
        scratch_shapes=[pltpu.VMEM((tm, N), F32)],
        name="out_proj_ln",
        compiler_params=_params("parallel", "arbitrary"),
    )(y, w, x, g.reshape(1, N), b.reshape(1, N))


def _ssd_body(xbc_ref, z_ref, dt_ref, conv0_ref, h0_ref, cw_ref, cbias_ref, dtb_ref, alog_ref,
              de_ref, nw_ref, e3_ref, tril_ref, mdiag_ref, mneg_ref, meven_ref,
              *rest, nsteps, nchunks, valid):
    y_ref, ht_ref, convt_ref, ext_ref, st_ref, ybuf_ref = rest[-6:]
    L = CHUNK
    W = SSD_INNER
    step = pl.program_id(1)
    total = nsteps * nchunks * L

    @pl.when(step == 0)
    def _():
        st_ref[...] = h0_ref[0, 0].T
        ext_ref[CARRY0:8, :] = conv0_ref[0]

    a_neg = -jnp.exp(alog_ref[...])

    def chunk(c, carry):
        r0 = pl.multiple_of(c * L, L)
        ext_ref[8:8 + L, :] = xbc_ref[0, pl.ds(r0, L), :]
        acc = cbias_ref[...] + ext_ref[CARRY0:CARRY0 + L, :] * cw_ref[0:1, :]
        for k in range(1, CONV_W):
            acc = acc + ext_ref[CARRY0 + k:CARRY0 + k + L, :] * cw_ref[k:k + 1, :]
        ext_ref[CARRY0:8, :] = ext_ref[CARRY0 + L:8 + L, :]
        xbc = _silu(acc)
        xs = xbc[:, :W]
        bm = xbc[:, W:W + SSD_G * SSD_N].astype(BF16)
        cm = xbc[:, W + SSD_G * SSD_N:].astype(BF16)

        dt = _softplus(dt_ref[0, pl.ds(r0, L), :] + dtb_ref[...])
        if valid < total:
            rows = lax.broadcasted_iota(jnp.int32, (L, SSD_H), 0) + (step * nchunks + c) * L
            dt = jnp.where(rows < valid, dt, 0.0)
        a = dt * a_neg
        tril = tril_ref[...]
        cum = sum(_dot(tril, p) for p in _split(a, 3))

        cum_e = _dot(jnp.concatenate(_split(cum, 3), axis=1), e3_ref[...])
        dt_e = _dot(jnp.concatenate(_split(dt, 2), axis=1), e3_ref[0:2 * SSD_H, :])

        cum_flat = jnp.sum(cum_e * mdiag_ref[...], axis=0, keepdims=True)
        decay = jnp.exp(cum_e - cum_flat + mneg_ref[...])

        xdt = xs * dt_e
        xdt_top = xdt * meven_ref[...]
        xdt_bot = (xdt - xdt_top).astype(BF16)
        xdt_top = xdt_top.astype(BF16)
        cum_last = cum_e[L - 1:L, :]
        exp_cum = jnp.exp(cum_e)
        xt = (xs * (jnp.exp(cum_last - cum_e) * dt_e)).astype(BF16)
        d_last = jnp.exp(cum_last)

        for g in range(SSD_G):
            gs = slice(g * SSD_N, (g + 1) * SSD_N)
            hs = slice(g * 512, (g + 1) * 512)
            cg = cm[:, gs]
            bg = bm[:, gs]
            cb2 = _dot_nt(cg, jnp.concatenate([bg, bg], axis=0))
            hg = st_ref[:, hs]
            ys = _dot(cg, hg.astype(BF16))
            for pr in range(4):
                lo = g * 512 + pr * LANE
                ps = slice(lo, lo + LANE)
                w = (cb2 * decay[:, ps]).astype(BF16)
                xblk = jnp.concatenate([xdt_top[:, ps], xdt_bot[:, ps]], axis=0)
                ybuf_ref[:, ps] = _dot(w, xblk) + ys[:, pr * LANE:(pr + 1) * LANE] * exp_cum[:, ps]
            st_ref[:, hs] = hg * d_last[:, hs] + _dot_tn(bg, xt[:, hs])

        y = ybuf_ref[...] + de_ref[...] * xs
        yg = y * _silu(z_ref[0, pl.ds(r0, L), :])
        for g in range(SSD_G):
            hs = slice(g * 512, (g + 1) * 512)
            blk = yg[:, hs]
            ms = jnp.mean(blk * blk, axis=-1, keepdims=True)
            y_ref[0, pl.ds(r0, L), hs] = (blk * lax.rsqrt(ms + RMS_EPS) * nw_ref[:, hs]).astype(BF16)
        return carry

    lax.fori_loop(0, nchunks, chunk, 0)

    @pl.when(step == nsteps - 1)
    def _():
        ht_ref[0, 0] = st_ref[...].T
        vl = valid - (total - L)
        convt_ref[0] = ext_ref[8 + vl - (CONV_W - 1):8 + vl, :]


def _ssd_scan(xbc, z, dt, conv0, h0_all, p, valid, layer, ht_all):
    B, T, _ = xbc.shape
    L = CHUNK
    assert SSD_P == L and T % L == 0 and valid > T - L and valid >= CONV_W - 1
    tb = min(T, 2 * L)
    nsteps = T // tb
    nchunks = tb // L
    W = SSD_INNER
    nl = h0_all.shape[0]
    eye = jnp.repeat(jnp.eye(SSD_H, dtype=BF16), SSD_P, axis=1)
    e3 = jnp.concatenate([eye, eye, eye], axis=0)
    tril = jnp.tril(jnp.ones((L, L), BF16))
    t_i = jnp.arange(L, dtype=jnp.int32)[:, None]
    s_i = (jnp.arange(W, dtype=jnp.int32) % L)[None, :]
    mdiag = (s_i == t_i).astype(F32)
    mneg = jnp.where(s_i <= t_i, 0.0, -jnp.inf).astype(F32)
    meven = ((jnp.arange(W, dtype=jnp.int32) // SSD_P) % 2 == 0).astype(F32).reshape(1, W)
    full = lambda shape: pl.BlockSpec(shape, lambda b, s: (0,) * len(shape))
    alias_specs, alias_args, aliases = [], [], {}
    if ht_all is not None:
        alias_specs, alias_args, aliases = [pl.BlockSpec(memory_space=pl.ANY)], [ht_all.reshape(nl, B, W, SSD_N)], {16: 1}
    y, ht, convt = pl.pallas_call(
        functools.partial(_ssd_body, nsteps=nsteps, nchunks=nchunks, valid=valid),
        grid=(B, nsteps),
        input_output_aliases=aliases,
        in_specs=[pl.BlockSpec((1, tb, SSD_CONV), lambda b, s: (b, s, 0)),
                  pl.BlockSpec((1, tb, W), lambda b, s: (b, s, 0)),
                  pl.BlockSpec((1, tb, SSD_H), lambda b, s: (b, s, 0)),
                  pl.BlockSpec((1, CONV_W - 1, SSD_CONV), lambda b, s: (b, 0, 0)),
                  pl.BlockSpec((1, 1, W, SSD_N), lambda b, s: (layer, b, 0, 0)),
                  full((CONV_W, SSD_CONV)), full((1, SSD_CONV)), full((1, SSD_H)), full((1, SSD_H)),
                  full((1, W)), full((1, W)), full((3 * SSD_H, W)), full((L, L)),
                  full((L, W)), full((L, W)), full((1, W))] + alias_specs,
        out_specs=[pl.BlockSpec((1, tb, W), lambda b, s: (b, s, 0)),
                   pl.BlockSpec((1, 1, W, SSD_N), lambda b, s: (layer, b, 0, 0)),
                   pl.BlockSpec((1, CONV_W - 1, SSD_CONV), lambda b, s: (b, 0, 0))],
        out_shape=[jax.ShapeDtypeStruct((B, T, W), BF16),
                   jax.ShapeDtypeStruct((nl, B, W, SSD_N), F32),
                   jax.ShapeDtypeStruct((B, CONV_W - 1, SSD_CONV), F32)],
        scratch_shapes=[pltpu.VMEM((8 + L, SSD_CONV), F32),
                        pltpu.VMEM((SSD_N, W), F32),
                        pltpu.VMEM((L, W), F32)],
        name="ssd_scan",
        compiler_params=_params("parallel", "arbitrary"),
    )(xbc, z, dt, conv0, h0_all.reshape(nl, B, W, SSD_N),
      p["conv_w"], p["conv_b"].reshape(1, -1), p["dt_bias"].reshape(1, -1), p["A_log"].reshape(1, -1),
      jnp.repeat(p["D"], SSD_P).reshape(1, W), p["norm_w"].reshape(1, W), e3, tril, mdiag, mneg, meven,
      *alias_args)
    return y, ht.reshape(nl, B, SSD_H, SSD_P, SSD_N), convt


def _ml_body(xm_ref, z_ref, o_ref, g_ref, gt_ref, conv0_ref, c0_ref, n0_ref, m0_ref, cw_ref, cbias_ref,
             wq_ref, wk_ref, wv_ref, gb_ref, gbt_ref, nw_ref, skip_ref, tril_ref, triu_ref,
             *rest, L, nsteps, nchunks, valid):
    y_ref, ct_ref, nt_ref, mt_ref, convt_ref, ext_ref, q_s, k_s, v_s, xc_s = rest[-10:]
    tb = nchunks * L
    total = nsteps * tb
    head = pl.program_id(0)
    step = pl.program_id(2)

    @pl.when(step == 0)
    def _():
        ct_ref[...] = c0_ref[...]
        nt_ref[...] = n0_ref[...]
        mt_ref[...] = m0_ref[...]
        ext_ref[CARRY0:8, :] = conv0_ref[0]

    xm = xm_ref[0]
    ext_ref[8:8 + tb, :] = xm
    acc = cbias_ref[...] + ext_ref[CARRY0:CARRY0 + tb, :] * cw_ref[0:1, :]
    for k in range(1, CONV_W):
        acc = acc + ext_ref[CARRY0 + k:CARRY0 + k + tb, :] * cw_ref[k:k + 1, :]
    ext_ref[CARRY0:8, :] = ext_ref[CARRY0 + tb:8 + tb, :]
    xc = _silu(acc)
    xc_s[...] = xc
    xcb = xc.astype(BF16)
    q_s[...] = _dot(xcb, wq_ref[0]) * (ML_D ** -0.5)
    k_s[...] = _dot(xcb, wk_ref[0])
    v_s[...] = _dot(xm.astype(BF16), wv_ref[0]).astype(BF16)

    @pl.when(step == nsteps - 1)
    def _():
        vl = valid - (total - tb)
        convt_ref[0] = ext_ref[8 + vl - (CONV_W - 1):8 + vl, :]

    def chunk(c, carry):
        r0 = pl.multiple_of(c * L, L)
        q = q_s[pl.ds(r0, L), :]
        k = k_s[pl.ds(r0, L), :]
        v = v_s[pl.ds(r0, L), :]
        qb = q.astype(BF16)

        gc = g_ref[0, pl.ds(r0, L), :] + gb_ref[...]
        li_c = gc
        lf_c = -_softplus(-gc)
        lane16 = lax.broadcasted_iota(jnp.int32, (L, 2 * ML_H), 1)
        if valid < total:
            rows = lax.broadcasted_iota(jnp.int32, (L, 2 * ML_H), 0) + (step * nchunks + c) * L
            li_c = jnp.where(rows < valid, li_c, NEG)
            lf_c = jnp.where(rows < valid, lf_c, 0.0)
        tril = tril_ref[...]
        b_c = sum(_dot(tril, p) for p in _split(lf_c, 3))
        li_col = jnp.sum(jnp.where(lane16 == head, li_c, 0.0), axis=1, keepdims=True)
        b_col = jnp.sum(jnp.where(lane16 == head + ML_H, b_c, 0.0), axis=1, keepdims=True)

        gr = gt_ref[0, c] + gbt_ref[...]
        li_r = gr
        lf_r = -_softplus(-gr)
        sub16 = lax.broadcasted_iota(jnp.int32, (2 * ML_H, L), 0)
        if valid < total:
            cols = lax.broadcasted_iota(jnp.int32, (2 * ML_H, L), 1) + (step * nchunks + c) * L
            li_r = jnp.where(cols < valid, li_r, NEG)
            lf_r = jnp.where(cols < valid, lf_r, 0.0)
        triu = triu_ref[...]
        b_r = sum(_dot(p, triu) for p in _split(lf_r, 3))
        li_row = jnp.sum(jnp.where(sub16 == head, li_r, 0.0), axis=0, keepdims=True)
        b_row = jnp.sum(jnp.where(sub16 == head + ML_H, b_r, 0.0), axis=0, keepdims=True)

        m_prev = mt_ref[0, 0]
        t_i = lax.broadcasted_iota(jnp.int32, (L, L), 0)
        s_i = lax.broadcasted_iota(jnp.int32, (L, L), 1)
        dm = jnp.where(s_i <= t_i, b_col - b_row + li_row, -jnp.inf)
        inter = b_col + m_prev
        m_t = jnp.maximum(inter, jnp.max(dm, axis=1, keepdims=True))
        s = _dot_nt(qb, k.astype(BF16)) * jnp.exp(dm - m_t)
        gg = jnp.exp(inter - m_t)
        c_prev = ct_ref[0, 0, 0]
        n_prev = nt_ref[0, 0]
        num = _dot(s.astype(BF16), v) + gg * _dot(qb, c_prev.astype(BF16))
        den = jnp.sum(s, axis=1, keepdims=True) + gg * jnp.sum(q * n_prev, axis=1, keepdims=True)
        hh = num / jnp.maximum(jnp.abs(den), jnp.exp(-m_t))
        m_new = m_t[L - 1:L, :]
        b_last = b_col[L - 1:L, :]
        tail = jnp.exp(b_last - b_col + li_col - m_new)
        dec = jnp.exp(b_last + m_prev - m_new)
        kt = k * tail
        ct_ref[0, 0, 0] = dec * c_prev + _dot_tn(kt.astype(BF16), v)
        nt_ref[0, 0] = dec * n_prev + jnp.sum(kt, axis=0, keepdims=True)
        mt_ref[0, 0] = m_new

        mu = jnp.mean(hh, axis=-1, keepdims=True)
        d = hh - mu
        var = jnp.mean(d * d, axis=-1, keepdims=True)
        hn = d * lax.rsqrt(var + LN_EPS) * nw_ref[0]
        og = jax.nn.sigmoid(o_ref[0, pl.ds(r0, L), :])
        hc = og * hn + skip_ref[...] * xc_s[pl.ds(r0, L), :]
        y_ref[0, pl.ds(r0, L), :] = (hc * _silu(z_ref[0, pl.ds(r0, L), :])).astype(BF16)
        return carry

    lax.fori_loop(0, nchunks, chunk, 0)


def _ml_scan(proj, gates, conv0, c0_all, n0, m0, p, valid, layer, ct_all):
    B, T, _ = proj.shape
    D = ML_D
    H = ML_H
    L = ML_CHUNK if T % ML_CHUNK == 0 else CHUNK
    tb = min(T, ML_BLOCK)
    assert T % tb == 0 and tb % L == 0 and valid > T - L and valid >= CONV_W - 1
    nsteps = T // tb
    nchunks = tb // L
    gates_t = gates.reshape(B, T // L, L, 2 * H).transpose(0, 1, 3, 2)
    gb = jnp.concatenate([p["b_i"], p["b_f"]]).reshape(1, 2 * H)
    tril = jnp.tril(jnp.ones((L, L), BF16))
    const = lambda shape: pl.BlockSpec(shape, lambda h, b, s: (0,) * len(shape))
    alias_specs, alias_args, aliases = [], [], {}
    if ct_all is not None:
        alias_specs, alias_args, aliases = [pl.BlockSpec(memory_space=pl.ANY)], [ct_all], {20: 1}
    y, ct, nt, mt, convt = pl.pallas_call(
        functools.partial(_ml_body, L=L, nsteps=nsteps, nchunks=nchunks, valid=valid),
        grid=(H, B, nsteps),
        input_output_aliases=aliases,
        in_specs=[pl.BlockSpec((1, tb, D), lambda h, b, s: (b, s, h)),
                  pl.BlockSpec((1, tb, D), lambda h, b, s: (b, s, H + h)),
                  pl.BlockSpec((1, tb, D), lambda h, b, s: (b, s, 2 * H + h)),
                  pl.BlockSpec((1, tb, 2 * H), lambda h, b, s: (b, s, 0)),
                  pl.BlockSpec((1, nchunks, 2 * H, L), lambda h, b, s: (b, s, 0, 0)),
                  pl.BlockSpec((1, CONV_W - 1, D), lambda h, b, s: (b, 0, h)),
                  pl.BlockSpec((1, 1, 1, D, D), lambda h, b, s: (layer, b, h, 0, 0)),
                  pl.BlockSpec((1, 1, 1, D), lambda h, b, s: (b, h, 0, 0)),
                  pl.BlockSpec((1, 1, 1, 1), lambda h, b, s: (b, h, 0, 0)),
                  pl.BlockSpec((CONV_W, D), lambda h, b, s: (0, h)),
                  pl.BlockSpec((1, D), lambda h, b, s: (0, h)),
                  pl.BlockSpec((1, D, D), lambda h, b, s: (h, 0, 0)),
                  pl.BlockSpec((1, D, D), lambda h, b, s: (h, 0, 0)),
                  pl.BlockSpec((1, D, D), lambda h, b, s: (h, 0, 0)),
                  const((1, 2 * H)), const((2 * H, 1)),
                  pl.BlockSpec((1, 1, D), lambda h, b, s: (h, 0, 0)),
                  pl.BlockSpec((1, D), lambda h, b, s: (0, h)),
                  const((L, L)), const((L, L))] + alias_specs,
        out_specs=[pl.BlockSpec((1, tb, D), lambda h, b, s: (b, s, h)),
                   pl.BlockSpec((1, 1, 1, D, D), lambda h, b, s: (layer, b, h, 0, 0)),
                   pl.BlockSpec((1, 1, 1, D), lambda h, b, s: (b, h, 0, 0)),
                   pl.BlockSpec((1, 1, 1, 1), lambda h, b, s: (b, h, 0, 0)),
                   pl.BlockSpec((1, CONV_W - 1, D), lambda h, b, s: (b, 0, h))],
        out_shape=[jax.ShapeDtypeStruct((B, T, ML_INNER), BF16),
                   jax.ShapeDtypeStruct(c0_all.shape, F32),
                   jax.ShapeDtypeStruct((B, H, 1, D), F32),
                   jax.ShapeDtypeStruct((B, H, 1, 1), F32),
                   jax.ShapeDtypeStruct((B, CONV_W - 1, ML_INNER), F32)],
        scratch_shapes=[pltpu.VMEM((8 + tb, D), F32),
                        pltpu.VMEM((tb, D), F32),
                        pltpu.VMEM((tb, D), F32),
                        pltpu.VMEM((tb, D), BF16),
                        pltpu.VMEM((tb, D), F32)],
        name="mlstm_scan",
        compiler_params=_params("parallel", "parallel", "arbitrary"),
    )(proj, proj, proj, gates, gates_t, conv0, c0_all, n0.reshape(B, H, 1, D), m0.reshape(B, H, 1, 1),
      p["conv_w"], p["conv_b"].reshape(1, -1), p["w_q"], p["w_k"], p["w_v"],
      gb, gb.reshape(2 * H, 1), p["norm_w"].reshape(H, 1, D), p["skip"].reshape(1, -1), tril, tril.T,
      *alias_args)
    return y, ct, nt.reshape(B, H, D), mt.reshape(B, H), convt


def _pad_frames(a, tp):
    t = a.shape[1]
    return a if t == tp else jnp.pad(a, ((0, 0), (0, tp - t), (0, 0)))


def _ssd_layer(xf, xb, conv0, h0_all, p, ln_g, ln_b, layer, ht_all):
    B, T, D = xf.shape
    M = B * T
    tp = -(-T // CHUNK) * CHUNK
    xb2 = xb.reshape(M, D)
    z = _pad_frames(_matmul(xb2, p["w_z"]).reshape(B, T, -1), tp)
    xbc = _pad_frames(_matmul(xb2, p["w_xbc"]).reshape(B, T, -1), tp)
    dt = _pad_frames(_matmul(xb2, p["w_dt"]).reshape(B, T, -1), tp)
    y, ht_all, convt = _ssd_scan(xbc, z, dt, conv0, h0_all, p, T, layer, ht_all)
    of, ob = _out_ln(y[:, :T].reshape(M, -1), p["w_out"], xf.reshape(M, D), ln_g, ln_b)
    return of.reshape(B, T, D), ob.reshape(B, T, D), convt, ht_all


def _ml_layer(xf, xb, conv0, c0_all, n0, m0, p, ln_g, ln_b, layer, ct_all):
    B, T, D = xf.shape
    M = B * T
    tp = -(-T // CHUNK) * CHUNK
    xb2 = xb.reshape(M, D)
    proj = _pad_frames(_matmul(xb2, p["w_main"]).reshape(B, T, -1), tp)
    gates = _pad_frames(_matmul(xb2, p["w_gate"]).reshape(B, T, -1), tp)
    y, ct_all, nt, mt, convt = _ml_scan(proj, gates, conv0, c0_all, n0, m0, p, T, layer, ct_all)
    of, ob = _out_ln(y[:, :T].reshape(M, -1), p["w_out"], xf.reshape(M, D), ln_g, ln_b)
    return of.reshape(B, T, D), ob.reshape(B, T, D), convt, ct_all, nt, mt


def _trunk(x, states, ssd_p, ml_p, ln_g, ln_b):
    ssd_conv, ssd_h, ml_conv, ml_c, ml_n, ml_m = states
    xf, xb = x, x.astype(BF16)
    o_sc, o_mc, o_mn, o_mm = [], [], [], []
    ht_all = ct_all = None
    for i in range(DEPTH):
        j = i // 2
        if i % 2 == 0:
            xf, xb, c, ht_all = _ssd_layer(xf, xb, ssd_conv[j], ssd_h, ssd_p[j], ln_g[i], ln_b[i], j, ht_all)
            o_sc.append(c)
        else:
            xf, xb, c, ct_all, n, m = _ml_layer(xf, xb, ml_conv[j], ml_c, ml_n[j], ml_m[j], ml_p[j],
                                                ln_g[i], ln_b[i], j, ct_all)
            o_mc.append(c)
            o_mn.append(n)
            o_mm.append(m)
    return (xf, jnp.stack(o_sc), ht_all, jnp.stack(o_mc), ct_all, jnp.stack(o_mn), jnp.stack(o_mm))


def kernel(x_prompt, x_sample, state_ssd_conv, state_ssd_h, state_mlstm_conv, state_mlstm_C, state_mlstm_n, state_mlstm_m, ssd_w_in, ssd_conv_w, ssd_conv_b, ssd_dt_bias, ssd_A_log, ssd_D, ssd_norm_w, ssd_w_out, ml_w_in, ml_conv_w, ml_conv_b, ml_w_q, ml_w_k, ml_w_v, ml_b_i, ml_b_f, ml_norm_w, ml_skip, ml_w_out, ln_g, ln_b):
    na, nb = ssd_w_in.shape[0], ml_w_in.shape[0]
    ssd_p = []
    for j in range(na):
        w = ssd_w_in[j].astype(BF16)
        ssd_p.append(dict(
            w_z=w[:, :SSD_INNER], w_xbc=w[:, SSD_INNER:SSD_INNER + SSD_CONV], w_dt=w[:, SSD_INNER + SSD_CONV:],
            conv_w=ssd_conv_w[j], conv_b=ssd_conv_b[j], dt_bias=ssd_dt_bias[j], A_log=ssd_A_log[j],
            D=ssd_D[j], norm_w=ssd_norm_w[j], w_out=ssd_w_out[j].astype(BF16)))
    ml_p = []
    for j in range(nb):
        w = ml_w_in[j].astype(BF16)
        ml_p.append(dict(
            w_main=w[:, :3 * ML_INNER], w_gate=w[:, 3 * ML_INNER:],
            conv_w=ml_conv_w[j], conv_b=ml_conv_b[j],
            w_q=ml_w_q[j].astype(BF16), w_k=ml_w_k[j].astype(BF16), w_v=ml_w_v[j].astype(BF16),
            b_i=ml_b_i[j], b_f=ml_b_f[j], norm_w=ml_norm_w[j], skip=ml_skip[j],
            w_out=ml_w_out[j].astype(BF16)))

    bp = x_prompt.shape[0]
    zero_states = (jnp.zeros((na, bp, CONV_W - 1, SSD_CONV), F32),
                   jnp.zeros((na, bp, SSD_H, SSD_P, SSD_N), F32),
                   jnp.zeros((nb, bp, CONV_W - 1, ML_INNER), F32),
                   jnp.zeros((nb, bp, ML_H, ML_D, ML_D), F32),
                   jnp.zeros((nb, bp, ML_H, ML_D), F32),
                   jnp.zeros((nb, bp, ML_H), F32))
    yp, p_sc, p_sh, p_mc, p_mC, p_mn, p_mm = _trunk(x_prompt, zero_states, ssd_p, ml_p, ln_g, ln_b)
    cache = (state_ssd_conv, state_ssd_h, state_mlstm_conv, state_mlstm_C, state_mlstm_n, state_mlstm_m)
    ys, s_sc, s_sh, s_mc, s_mC, s_mn, s_mm = _trunk(x_sample, cache, ssd_p, ml_p, ln_g, ln_b)
    return (yp, ys, p_sc, p_sh, p_mc, p_mC, p_mn, p_mm, s_sc, s_sh, s_mc, s_mC, s_mn, s_mm)
```

```python
import functools

import jax
import jax.numpy as jnp
from jax import lax
from jax.experimental import pallas as pl
from jax.experimental.pallas import tpu as pltpu

F32 = jnp.float32
BF16 = jnp.bfloat16

D_MODEL = 2048
DEPTH = 4
CHUNK = 64
ML_CHUNK = 256
ML_BLOCK = 512
CONV_W = 4
SSD_INNER = 4096
SSD_P = 64
SSD_H = 64
SSD_G = 8
SSD_N = 128
SSD_CONV = SSD_INNER + 2 * SSD_G * SSD_N
ML_INNER = 4096
ML_H = 8
ML_D = 512
ALPHA = (2 * DEPTH) ** 0.25
LN_EPS = 1e-5
RMS_EPS = 1e-6
NEG = -1e30

LANE = 128
VMEM_LIMIT = 56 * 1024 * 1024
CARRY0 = 8 - (CONV_W - 1)


def _params(*sem):
    return pltpu.CompilerParams(dimension_semantics=sem, vmem_limit_bytes=VMEM_LIMIT)


def _dot(a, b):
    return jnp.dot(a, b, preferred_element_type=F32)


def _dot_nt(a, b):
    return lax.dot_general(a, b, (((1,), (1,)), ((), ())), preferred_element_type=F32)


def _dot_tn(a, b):
    return lax.dot_general(a, b, (((0,), (0,)), ((), ())), preferred_element_type=F32)


def _split(x, n):
    out = []
    r = x
    for i in range(n):
        p = r.astype(BF16)
        out.append(p)
        if i + 1 < n:
            r = r - p.astype(F32)
    return out


def _silu(x):
    return x * jax.nn.sigmoid(x)


def _softplus(x):
    return jnp.maximum(x, 0.0) + jnp.log1p(jnp.exp(-jnp.abs(x)))


def _mm_body(a_ref, w_ref, o_ref):
    o_ref[...] = _dot(a_ref[...], w_ref[...]).astype(o_ref.dtype)


def _matmul(a, w, out_dtype=F32):
    M, K = a.shape
    N = w.shape[1]
    tm = min(M, 1024)
    tn = 1024 if N % 1024 == 0 else min(N, 512)
    assert M % tm == 0 and N % tn == 0
    return pl.pallas_call(
        _mm_body,
        grid=(M // tm, N // tn),
        in_specs=[pl.BlockSpec((tm, K), lambda i, j: (i, 0)),
                  pl.BlockSpec((K, tn), lambda i, j: (0, j))],
        out_specs=pl.BlockSpec((tm, tn), lambda i, j: (i, j)),
        out_shape=jax.ShapeDtypeStruct((M, N), out_dtype),
        name="in_proj",
        compiler_params=_params("parallel", "arbitrary"),
    )(a, w)


def _out_body(y_ref, w_ref, x_ref, g_ref, b_ref, of_ref, ob_ref, acc_ref, *, nk):
    k = pl.program_id(1)

    @pl.when(k == 0)
    def _():
        acc_ref[...] = jnp.zeros_like(acc_ref)

    acc_ref[...] += _dot(y_ref[...], w_ref[...])

    @pl.when(k == nk - 1)
    def _():
        r = ALPHA * x_ref[...] + acc_ref[...]
        mu = jnp.mean(r, axis=-1, keepdims=True)
        d = r - mu
        var = jnp.mean(d * d, axis=-1, keepdims=True)
        o = d * lax.rsqrt(var + LN_EPS) * g_ref[...] + b_ref[...]
        of_ref[...] = o
        ob_ref[...] = o.astype(BF16)


def _out_ln(y, w, x, g, b):
    M, K = y.shape
    N = w.shape[1]
    tm = min(M, 512)
    tk = min(K, 2048)
    nk = K // tk
    return pl.pallas_call(
        functools.partial(_out_body, nk=nk),
        grid=(M // tm, nk),
        in_specs=[pl.BlockSpec((tm, tk), lambda i, k: (i, k)),
                  pl.BlockSpec((tk, N), lambda i, k: (k, 0)),
                  pl.BlockSpec((tm, N), lambda i, k: (i, 0)),
                  pl.BlockSpec((1, N), lambda i, k: (0, 0)),
                  pl.BlockSpec((1, N), lambda i, k: (0, 0))],
        out_specs=[pl.BlockSpec((tm, N), lambda i, k: (i, 0)),
                   pl.BlockSpec((tm, N), lambda i, k: (i, 0))],
        out_shape=[jax.ShapeDtypeStruct((M, N), F32), jax.ShapeDtypeStruct((M, N), BF16)],
        scratch_shapes=[pltpu.VMEM((tm, N), F32)],
        name="out_proj_ln",
        compiler_params=_params("parallel", "arbitrary"),
    )(y, w, x, g.reshape(1, N), b.reshape(1, N))


def _ssd_body(xbc_ref, z_ref, dt_ref, conv0_ref, h0_ref, cw_ref, cbias_ref, dtb_ref, alog_ref,
              de_ref, nw_ref, e3_ref, tril_ref, mdiag_ref, mneg_ref, meven_ref,
              *rest, nsteps, nchunks, valid):
    y_ref, ht_ref, convt_ref, ext_ref, st_ref, ybuf_ref = rest[-6:]
    L = CHUNK
    W = SSD_INNER
    step = pl.program_id(1)
    total = nsteps * nchunks * L

    @pl.when(step == 0)
    def _():
        st_ref[...] = h0_ref[0, 0].T
        ext_ref[CARRY0:8, :] = conv0_ref[0]

    a_neg = -jnp.exp(alog_ref[...])

    def chunk(c, carry):
        r0 = pl.multiple_of(c * L, L)
        ext_ref[8:8 + L, :] = xbc_ref[0, pl.ds(r0, L), :]
        acc = cbias_ref[...] + ext_ref[CARRY0:CARRY0 + L, :] * cw_ref[0:1, :]
        for k in range(1, CONV_W):
            acc = acc + ext_ref[CARRY0 + k:CARRY0 + k + L, :] * cw_ref[k:k + 1, :]
        ext_ref[CARRY0:8, :] = ext_ref[CARRY0 + L:8 + L, :]
        xbc = _silu(acc)
        xs = xbc[:, :W]
        bm = xbc[:, W:W + SSD_G * SSD_N].astype(BF16)
        cm = xbc[:, W + SSD_G * SSD_N:].astype(BF16)

        dt = _softplus(dt_ref[0, pl.ds(r0, L), :] + dtb_ref[...])
        if valid < total:
            rows = lax.broadcasted_iota(jnp.int32, (L, SSD_H), 0) + (step * nchunks + c) * L
            dt = jnp.where(rows < valid, dt, 0.0)
        a = dt * a_neg
        tril = tril_ref[...]
        cum = sum(_dot(tril, p) for p in _split(a, 3))

        cum_e = _dot(jnp.concatenate(_split(cum, 3), axis=1), e3_ref[...])
        dt_e = _dot(jnp.concatenate(_split(dt, 2), axis=1), e3_ref[0:2 * SSD_H, :])

        cum_flat = jnp.sum(cum_e * mdiag_ref[...], axis=0, keepdims=True)
        decay = jnp.exp(cum_e - cum_flat + mneg_ref[...])

        xdt = xs * dt_e
        xdt_top = xdt * meven_ref[...]
        xdt_bot = (xdt - xdt_top).astype(BF16)
        xdt_top = xdt_top.astype(BF16)
        cum_last = cum_e[L - 1:L, :]
        exp_cum = jnp.exp(cum_e)
        xt = (xs * (jnp.exp(cum_last - cum_e) * dt_e)).astype(BF16)
        d_last = jnp.exp(cum_last)

        for g in range(SSD_G):
            gs = slice(g * SSD_N, (g + 1) * SSD_N)
            hs = slice(g * 512, (g + 1) * 512)
            cg = cm[:, gs]
            bg = bm[:, gs]
            cb2 = _dot_nt(cg, jnp.concatenate([bg, bg], axis=0))
            hg = st_ref[:, hs]
            ys = _dot(cg, hg.astype(BF16))
            for pr in range(4):
                lo = g * 512 + pr * LANE
                ps = slice(lo, lo + LANE)
                w = (cb2 * decay[:, ps]).astype(BF16)
                xblk = jnp.concatenate([xdt_top[:, ps], xdt_bot[:, ps]], axis=0)
                ybuf_ref[:, ps] = _dot(w, xblk) + ys[:, pr * LANE:(pr + 1) * LANE] * exp_cum[:, ps]
            st_ref[:, hs] = hg * d_last[:, hs] + _dot_tn(bg, xt[:, hs])

        y = ybuf_ref[...] + de_ref[...] * xs
        yg = y * _silu(z_ref[0, pl.ds(r0, L), :])
        for g in range(SSD_G):
            hs = slice(g * 512, (g + 1) * 512)
            blk = yg[:, hs]
            ms = jnp.mean(blk * blk, axis=-1, keepdims=True)
            y_ref[0, pl.ds(r0, L), hs] = (blk * lax.rsqrt(ms + RMS_EPS) * nw_ref[:, hs]).astype(BF16)
        return carry

    lax.fori_loop(0, nchunks, chunk, 0)

    @pl.when(step == nsteps - 1)
    def _():
        ht_ref[0, 0] = st_ref[...].T
        vl = valid - (total - L)
        convt_ref[0] = ext_ref[8 + vl - (CONV_W - 1):8 + vl, :]


def _ssd_scan(xbc, z, dt, conv0, h0_all, p, valid, layer, ht_all):
    B, T, _ = xbc.shape
    L = CHUNK
    assert SSD_P == L and T % L == 0 and valid > T - L and valid >= CONV_W - 1
    tb = min(T, 2 * L)
    nsteps = T // tb
    nchunks = tb // L
    W = SSD_INNER
    nl = h0_all.shape[0]
    eye = jnp.repeat(jnp.eye(SSD_H, dtype=BF16), SSD_P, axis=1)
    e3 = jnp.concatenate([eye, eye, eye], axis=0)
    tril = jnp.tril(jnp.ones((L, L), BF16))
    t_i = jnp.arange(L, dtype=jnp.int32)[:, None]
    s_i = (jnp.arange(W, dtype=jnp.int32) % L)[None, :]
    mdiag = (s_i == t_i).astype(F32)
    mneg = jnp.where(s_i <= t_i, 0.0, -jnp.inf).astype(F32)
    meven = ((jnp.arange(W, dtype=jnp.int32) // SSD_P) % 2 == 0).astype(F32).reshape(1, W)
    full = lambda shape: pl.BlockSpec(shape, lambda b, s: (0,) * len(shape))
    alias_specs, alias_args, aliases = [], [], {}
    if ht_all is not None:
        alias_specs, alias_args, aliases = [pl.BlockSpec(memory_space=pl.ANY)], [ht_all.reshape(nl, B, W, SSD_N)], {16: 1}
    y, ht, convt = pl.pallas_call(
        functools.partial(_ssd_body, nsteps=nsteps, nchunks=nchunks, valid=valid),
        grid=(B, nsteps),
        input_output_aliases=aliases,
        in_specs=[pl.BlockSpec((1, tb, SSD_CONV), lambda b, s: (b, s, 0)),
                  pl.BlockSpec((1, tb, W), lambda b, s: (b, s, 0)),
                  pl.BlockSpec((1, tb, SSD_H), lambda b, s: (b, s, 0)),
                  pl.BlockSpec((1, CONV_W - 1, SSD_CONV), lambda b, s: (b, 0, 0)),
                  pl.BlockSpec((1, 1, W, SSD_N), lambda b, s: (layer, b, 0, 0)),
                  full((CONV_W, SSD_CONV)), full((1, SSD_CONV)), full((1, SSD_H)), full((1, SSD_H)),
                  full((1, W)), full((1, W)), full((3 * SSD_H, W)), full((L, L)),
                  full((L, W)), full((L, W)), full((1, W))] + alias_specs,
        out_specs=[pl.BlockSpec((1, tb, W), lambda b, s: (b, s, 0)),
                   pl.BlockSpec((1, 1, W, SSD_N), lambda b, s: (layer, b, 0, 0)),
                   pl.BlockSpec((1, CONV_W - 1, SSD_CONV), lambda b, s: (b, 0, 0))],
        out_shape=[jax.ShapeDtypeStruct((B, T, W), BF16),
                   jax.ShapeDtypeStruct((nl, B, W, SSD_N), F32),
                   jax.ShapeDtypeStruct((B, CONV_W - 1, SSD_CONV), F32)],
        scratch_shapes=[pltpu.VMEM((8 + L, SSD_CONV), F32),
                        pltpu.VMEM((SSD_N, W), F32),
                        pltpu.VMEM((L, W), F32)],
        name="ssd_scan",
        compiler_params=_params("parallel", "arbitrary"),
    )(xbc, z, dt, conv0, h0_all.reshape(nl, B, W, SSD_N),
      p["conv_w"], p["conv_b"].reshape(1, -1), p["dt_bias"].reshape(1, -1), p["A_log"].reshape(1, -1),
      jnp.repeat(p["D"], SSD_P).reshape(1, W), p["norm_w"].reshape(1, W), e3, tril, mdiag, mneg, meven,
      *alias_args)
    return y, ht.reshape(nl, B, SSD_H, SSD_P, SSD_N), convt


def _ml_body(xm_ref, z_ref, o_ref, g_ref, gt_ref, conv0_ref, c0_ref, n0_ref, m0_ref, cw_ref, cbias_ref,
             wq_ref, wk_ref, wv_ref, gb_ref, gbt_ref, nw_ref, skip_ref, tril_ref, triu_ref,
             *rest, L, nsteps, nchunks, valid):
    y_ref, ct_ref, nt_ref, mt_ref, convt_ref, ext_ref, q_s, k_s, v_s, xc_s = rest[-10:]
    tb = nchunks * L
    total = nsteps * tb
    head = pl.program_id(0)
    step = pl.program_id(2)

    @pl.when(step == 0)
    def _():
        ct_ref[...] = c0_ref[...]
        nt_ref[...] = n0_ref[...]
        mt_ref[...] = m0_ref[...]
        ext_ref[CARRY0:8, :] = conv0_ref[0]

    xm = xm_ref[0]
    ext_ref[8:8 + tb, :] = xm
    acc = cbias_ref[...] + ext_ref[CARRY0:CARRY0 + tb, :] * cw_ref[0:1, :]
    for k in range(1, CONV_W):
        acc = acc + ext_ref[CARRY0 + k:CARRY0 + k + tb, :] * cw_ref[k:k + 1, :]
    ext_ref[CARRY0:8, :] = ext_ref[CARRY0 + tb:8 + tb, :]
    xc = _silu(acc)
    xc_s[...] = xc
    xcb = xc.astype(BF16)
    q_s[...] = _dot(xcb, wq_ref[0]) * (ML_D ** -0.5)
    k_s[...] = _dot(xcb, wk_ref[0])
    v_s[...] = _dot(xm.astype(BF16), wv_ref[0]).astype(BF16)

    @pl.when(step == nsteps - 1)
    def _():
        vl = valid - (total - tb)
        convt_ref[0] = ext_ref[8 + vl - (CONV_W - 1):8 + vl, :]

    def chunk(c, carry):
        r0 = pl.multiple_of(c * L, L)
        q = q_s[pl.ds(r0, L), :]
        k = k_s[pl.ds(r0, L), :]
        v = v_s[pl.ds(r0, L), :]
        qb = q.astype(BF16)

        gc = g_ref[0, pl.ds(r0, L), :] + gb_ref[...]
        li_c = gc
        lf_c = -_softplus(-gc)
        lane16 = lax.broadcasted_iota(jnp.int32, (L, 2 * ML_H), 1)
        if valid < total:
            rows = lax.broadcasted_iota(jnp.int32, (L, 2 * ML_H), 0) + (step * nchunks + c) * L
            li_c = jnp.where(rows < valid, li_c, NEG)
            lf_c = jnp.where(rows < valid, lf_c, 0.0)
        tril = tril_ref[...]
        b_c = sum(_dot(tril, p) for p in _split(lf_c, 3))
        li_col = jnp.sum(jnp.where(lane16 == head, li_c, 0.0), axis=1, keepdims=True)
        b_col = jnp.sum(jnp.where(lane16 == head + ML_H, b_c, 0.0), axis=1, keepdims=True)

        gr = gt_ref[0, c] + gbt_ref[...]
        li_r = gr
        lf_r = -_softplus(-gr)
        sub16 = lax.broadcasted_iota(jnp.int32, (2 * ML_H, L), 0)
        if valid < total:
            cols = lax.broadcasted_iota(jnp.int32, (2 * ML_H, L), 1) + (step * nchunks + c) * L
            li_r = jnp.where(cols < valid, li_r, NEG)
            lf_r = jnp.where(cols < valid, lf_r, 0.0)
        triu = triu_ref[...]
        b_r = sum(_dot(p, triu) for p in _split(lf_r, 3))
        li_row = jnp.sum(jnp.where(sub16 == head, li_r, 0.0), axis=0, keepdims=True)
        b_row = jnp.sum(jnp.where(sub16 == head + ML_H, b_r, 0.0), axis=0, keepdims=True)

        m_prev = mt_ref[0, 0]
        t_i = lax.broadcasted_iota(jnp.int32, (L, L), 0)
        s_i = lax.broadcasted_iota(jnp.int32, (L, L), 1)
        dm = jnp.where(s_i <= t_i, b_col - b_row + li_row, -jnp.inf)
        inter = b_col + m_prev
        m_t = jnp.maximum(inter, jnp.max(dm, axis=1, keepdims=True))
        s = _dot_nt(qb, k.astype(BF16)) * jnp.exp(dm - m_t)
        gg = jnp.exp(inter - m_t)
        c_prev = ct_ref[0, 0, 0]
        n_prev = nt_ref[0, 0]
        num = _dot(s.astype(BF16), v) + gg * _dot(qb, c_prev.astype(BF16))
        den = jnp.sum(s, axis=1, keepdims=True) + gg * jnp.sum(q * n_prev, axis=1, keepdims=True)
        hh = num / jnp.maximum(jnp.abs(den), jnp.exp(-m_t))
        m_new = m_t[L - 1:L, :]
        b_last = b_col[L - 1:L, :]
        tail = jnp.exp(b_last - b_col + li_col - m_new)
        dec = jnp.exp(b_last + m_prev - m_new)
        kt = k * tail
        ct_ref[0, 0, 0] = dec * c_prev + _dot_tn(kt.astype(BF16), v)
        nt_ref[0, 0] = dec * n_prev + jnp.sum(kt, axis=0, keepdims=True)
        mt_ref[0, 0] = m_new

        mu = jnp.mean(hh, axis=-1, keepdims=True)
        d = hh - mu
        var = jnp.mean(d * d, axis=-1, keepdims=True)
        hn = d * lax.rsqrt(var + LN_EPS) * nw_ref[0]
        og = jax.nn.sigmoid(o_ref[0, pl.ds(r0, L), :])
        hc = og * hn + skip_ref[...] * xc_s[pl.ds(r0, L), :]
        y_ref[0, pl.ds(r0, L), :] = (hc * _silu(z_ref[0, pl.ds(r0, L), :])).astype(BF16)
        return carry

    lax.fori_loop(0, nchunks, chunk, 0)


def _ml_scan(proj, gates, conv0, c0_all, n0, m0, p, valid, layer, ct_all):
    B, T, _ = proj.shape
    D = ML_D
    H = ML_H
    L = ML_CHUNK if T % ML_CHUNK == 0 else CHUNK
    tb = min(T, ML_BLOCK)
    assert T % tb == 0 and tb % L == 0 and valid > T - L and valid >= CONV_W - 1
    nsteps = T // tb
    nchunks = tb // L
    gates_t = gates.reshape(B, T // L, L, 2 * H).transpose(0, 1, 3, 2)
    gb = jnp.concatenate([p["b_i"], p["b_f"]]).reshape(1, 2 * H)
    tril = jnp.tril(jnp.ones((L, L), BF16))
    const = lambda shape: pl.BlockSpec(shape, lambda h, b, s: (0,) * len(shape))
    alias_specs, alias_args, aliases = [], [], {}
    if ct_all is not None:
        alias_specs, alias_args, aliases = [pl.BlockSpec(memory_space=pl.ANY)], [ct_all], {20: 1}
    y, ct, nt, mt, convt = pl.pallas_call(
        functools.partial(_ml_body, L=L, nsteps=nsteps, nchunks=nchunks, valid=valid),
        grid=(H, B, nsteps),
        input_output_aliases=aliases,
        in_specs=[pl.BlockSpec((1, tb, D), lambda h, b, s: (b, s, h)),
                  pl.BlockSpec((1, tb, D), lambda h, b, s: (b, s, H + h)),
                  pl.BlockSpec((1, tb, D), lambda h, b, s: (b, s, 2 * H + h)),
                  pl.BlockSpec((1, tb, 2 * H), lambda h, b, s: (b, s, 0)),
                  pl.BlockSpec((1, nchunks, 2 * H, L), lambda h, b, s: (b, s, 0, 0)),
                  pl.BlockSpec((1, CONV_W - 1, D), lambda h, b, s: (b, 0, h)),
                  pl.BlockSpec((1, 1, 1, D, D), lambda h, b, s: (layer, b, h, 0, 0)),
                  pl.BlockSpec((1, 1, 1, D), lambda h, b, s: (b, h, 0, 0)),
                  pl.BlockSpec((1, 1, 1, 1), lambda h, b, s: (b, h, 0, 0)),
                  pl.BlockSpec((CONV_W, D), lambda h, b, s: (0, h)),
                  pl.BlockSpec((1, D), lambda h, b, s: (0, h)),
                  pl.BlockSpec((1, D, D), lambda h, b, s: (h, 0, 0)),
                  pl.BlockSpec((1, D, D), lambda h, b, s: (h, 0, 0)),
                  pl.BlockSpec((1, D, D), lambda h, b, s: (h, 0, 0)),
                  const((1, 2 * H)), const((2 * H, 1)),
                  pl.BlockSpec((1, 1, D), lambda h, b, s: (h, 0, 0)),
                  pl.BlockSpec((1, D), lambda h, b, s: (0, h)),
                  const((L, L)), const((L, L))] + alias_specs,
        out_specs=[pl.BlockSpec((1, tb, D), lambda h, b, s: (b, s, h)),
                   pl.BlockSpec((1, 1, 1, D, D), lambda h, b, s: (layer, b, h, 0, 0)),
                   pl.BlockSpec((1, 1, 1, D), lambda h, b, s: (b, h, 0, 0)),
                   pl.BlockSpec((1, 1, 1, 1), lambda h, b, s: (b, h, 0, 0)),
                   pl.BlockSpec((1, CONV_W - 1, D), lambda h, b, s: (b, 0, h))],
        out_shape=[jax.ShapeDtypeStruct((B, T, ML_INNER), BF16),
                   jax.ShapeDtypeStruct(c0_all.shape, F32),
                   jax.ShapeDtypeStruct((B, H, 1, D), F32),
                   jax.ShapeDtypeStruct((B, H, 1, 1), F32),
                   jax.ShapeDtypeStruct((B, CONV_W - 1, ML_INNER), F32)],
        scratch_shapes=[pltpu.VMEM((8 + tb, D), F32),
                        pltpu.VMEM((tb, D), F32),
                        pltpu.VMEM((tb, D), F32),
                        pltpu.VMEM((tb, D), BF16),
                        pltpu.VMEM((tb, D), F32)],
        name="mlstm_scan",
        compiler_params=_params("parallel", "parallel", "arbitrary"),
    )(proj, proj, proj, gates, gates_t, conv0, c0_all, n0.reshape(B, H, 1, D), m0.reshape(B, H, 1, 1),
      p["conv_w"], p["conv_b"].reshape(1, -1), p["w_q"], p["w_k"], p["w_v"],
      gb, gb.reshape(2 * H, 1), p["norm_w"].reshape(H, 1, D), p["skip"].reshape(1, -1), tril, tril.T,
      *alias_args)
    return y, ct, nt.reshape(B, H, D), mt.reshape(B, H), convt


def _pad_frames(a, tp):
    t = a.shape[1]
    return a if t == tp else jnp.pad(a, ((0, 0), (0, tp - t), (0, 0)))


def _ssd_layer(xf, xb, conv0, h0_all, p, ln_g, ln_b, layer, ht_all):
    B, T, D = xf.shape
    M = B * T
    tp = -(-T // CHUNK) * CHUNK
    xb2 = xb.reshape(M, D)
    z = _pad_frames(_matmul(xb2, p["w_z"]).reshape(B, T, -1), tp)
    xbc = _pad_frames(_matmul(xb2, p["w_xbc"]).reshape(B, T, -1), tp)
    dt = _pad_frames(_matmul(xb2, p["w_dt"]).reshape(B, T, -1), tp)
    y, ht_all, convt = _ssd_scan(xbc, z, dt, conv0, h0_all, p, T, layer, ht_all)
    of, ob = _out_ln(y[:, :T].reshape(M, -1), p["w_out"], xf.reshape(M, D), ln_g, ln_b)
    return of.reshape(B, T, D), ob.reshape(B, T, D), convt, ht_all


def _ml_layer(xf, xb, conv0, c0_all, n0, m0, p, ln_g, ln_b, layer, ct_all):
    B, T, D = xf.shape
    M = B * T
    tp = -(-T // CHUNK) * CHUNK
    xb2 = xb.reshape(M, D)
    proj = _pad_frames(_matmul(xb2, p["w_main"]).reshape(B, T, -1), tp)
    gates = _pad_frames(_matmul(xb2, p["w_gate"]).reshape(B, T, -1), tp)
    y, ct_all, nt, mt, convt = _ml_scan(proj, gates, conv0, c0_all, n0, m0, p, T, layer, ct_all)
    of, ob = _out_ln(y[:, :T].reshape(M, -1), p["w_out"], xf.reshape(M, D), ln_g, ln_b)
    return of.reshape(B, T, D), ob.reshape(B, T, D), convt, ct_all, nt, mt


def _trunk(x, states, ssd_p, ml_p, ln_g, ln_b):
    ssd_conv, ssd_h, ml_conv, ml_c, ml_n, ml_m = states
    xf, xb = x, x.astype(BF16)
    o_sc, o_mc, o_mn, o_mm = [], [], [], []
    ht_all = ct_all = None
    for i in range(DEPTH):
        j = i // 2
        if i % 2 == 0:
            xf, xb, c, ht_all = _ssd_layer(xf, xb, ssd_conv[j], ssd_h, ssd_p[j], ln_g[i], ln_b[i], j, ht_all)
            o_sc.append(c)
        else:
            xf, xb, c, ct_all, n, m = _ml_layer(xf, xb, ml_conv[j], ml_c, ml_n[j], ml_m[j], ml_p[j],
                                                ln_g[i], ln_b[i], j, ct_all)
            o_mc.append(c)
            o_mn.append(n)
            o_mm.append(m)
    return (xf, jnp.stack(o_sc), ht_all, jnp.stack(o_mc), ct_all, jnp.stack(o_mn), jnp.stack(o_mm))


def kernel(x_prompt, x_sample, state_ssd_conv, state_ssd_h, state_mlstm_conv, state_mlstm_C, state_mlstm_n, state_mlstm_m, ssd_w_in, ssd_conv_w, ssd_conv_b, ssd_dt_bias, ssd_A_log, ssd_D, ssd_norm_w, ssd_w_out, ml_w_in, ml_conv_w, ml_conv_b, ml_w_q, ml_w_k, ml_w_v, ml_b_i, ml_b_f, ml_norm_w, ml_skip, ml_w_out, ln_g, ln_b):
    na, nb = ssd_w_in.shape[0], ml_w_in.shape[0]
    ssd_p = []
    for j in range(na):
        w = ssd_w_in[j].astype(BF16)
        ssd_p.append(dict(
            w_z=w[:, :SSD_INNER], w_xbc=w[:, SSD_INNER:SSD_INNER + SSD_CONV], w_dt=w[:, SSD_INNER + SSD_CONV:],
            conv_w=ssd_conv_w[j], conv_b=ssd_conv_b[j], dt_bias=ssd_dt_bias[j], A_log=ssd_A_log[j],
            D=ssd_D[j], norm_w=ssd_norm_w[j], w_out=ssd_w_out[j].astype(BF16)))
    ml_p = []
    for j in range(nb):
        w = ml_w_in[j].astype(BF16)
        ml_p.append(dict(
            w_main=w[:, :3 * ML_INNER], w_gate=w[:, 3 * ML_INNER:],
            conv_w=ml_conv_w[j], conv_b=ml_conv_b[j],
            w_q=ml_w_q[j].astype(BF16), w_k=ml_w_k[j].astype(BF16), w_v=ml_w_v[j].astype(BF16),
            b_i=ml_b_i[j], b_f=ml_b_f[j], norm_w=ml_norm_w[j], skip=ml_skip[j],
            w_out=ml_w_out[j].astype(BF16)))

    bp = x_prompt.shape[0]
    zero_states = (jnp.zeros((na, bp, CONV_W - 1, SSD_CONV), F32),
                   jnp.zeros((na, bp, SSD_H, SSD_P, SSD_N), F32),
                   jnp.zeros((nb, bp, CONV_W - 1, ML_INNER), F32),
                   jnp.zeros((nb, bp, ML_H, ML_D, ML_D), F32),
                   jnp.zeros((nb, bp, ML_H, ML_D), F32),
                   jnp.zeros((nb, bp, ML_H), F32))
    yp, p_sc, p_sh, p_mc, p_mC, p_mn, p_mm = _trunk(x_prompt, zero_states, ssd_p, ml_p, ln_g, ln_b)
    cache = (state_ssd_conv, state_ssd_h, state_mlstm_conv, state_mlstm_C, state_mlstm_n, state_mlstm_m)
    ys, s_sc, s_sh, s_mc, s_mC, s_mn, s_mm = _trunk(x_sample, cache, ssd_p, ml_p, ln_g, ln_b)
    return (yp, ys, p_sc, p_sh, p_mc, p_mC, p_mn, p_mm, s_sc, s_sh, s_mc, s_mC, s_mn, s_mm)
```
